```python
import math
import jax, jax.numpy as jnp
from jax import lax
import numpy as np

D_MODEL = 2048
BATCH = 16
SEQ = 256
DEPTH = 2
DEC_BATCH = 2
DEC_SEQ = 4096
PAST_LEN = 256

GRID_W = 64
CHUNK = 128
Q_BLOCK = 128
N_HEADS_A = 8
D_HEAD_A = 64
D_V_A = 2 * D_HEAD_A
N_GROUPS_B = 8
D_GROUP_B = 128
N_HEADS_R = 8
D_K_R = 64
D_V_R = 128
ROPE_DIM = 64
ROPE_BASE = 10000.0
N_EXPERT_GROUPS = 4
EXPERTS_PER_GROUP = 4
N_EXPERTS = N_EXPERT_GROUPS * EXPERTS_PER_GROUP
TOP_K_INNER = 2
D_EXPERT = 256
N_MOD = 6
EPS = 1e-6

W_QA = N_HEADS_A * 2 * D_HEAD_A
W_VA = N_HEADS_A * D_V_A
W_B = N_GROUPS_B * D_GROUP_B
W_QR = N_HEADS_R * D_K_R
W_VR = N_HEADS_R * D_V_R
IN_WIDTHS = (W_QA, W_QA, W_VA, W_B, W_B, W_QR, W_QR, W_VR, W_VR, D_MODEL, D_MODEL, D_MODEL)
D_IN = sum(IN_WIDTHS)

kernel_name = 'hybrid_diffusion_prefix_trunk_step'


def rmsnorm(x, g):
    xf = x.astype(jnp.float32)
    y = xf * lax.rsqrt(jnp.mean(xf * xf, axis=-1, keepdims=True) + EPS)
    return (y * g.astype(jnp.float32)).astype(x.dtype)


def axial_rope(n_tok):
    rows = n_tok // GRID_W
    row = jnp.repeat(jnp.arange(rows, dtype=jnp.float32), GRID_W)
    col = jnp.tile(jnp.arange(GRID_W, dtype=jnp.float32), rows)
    n_freq = ROPE_DIM // 4
    inv = ROPE_BASE ** (-jnp.arange(n_freq, dtype=jnp.float32) / n_freq)
    ang = jnp.concatenate([row[:, None] * inv, col[:, None] * inv], axis=-1)
    return jnp.cos(ang), jnp.sin(ang)


def apply_rope(x, cos, sin):
    shape = (1, cos.shape[0]) + (1,) * (x.ndim - 3) + (cos.shape[-1],)
    c = cos.reshape(shape).astype(x.dtype)
    s = sin.reshape(shape).astype(x.dtype)
    x1, x2 = jnp.split(x, 2, axis=-1)
    return jnp.concatenate([x1 * c - x2 * s, x1 * s + x2 * c], axis=-1)


def diff_attention(q, k, v, lam):
    b, t = q.shape[:2]
    nb = t // Q_BLOCK
    qb = jnp.moveaxis(q.reshape(b, nb, Q_BLOCK, N_HEADS_A, 2, D_HEAD_A), 1, 0)
    scale = D_HEAD_A ** -0.5

    def block(qblk):
        s = jnp.einsum('bqhmd,bkhmd->bhmqk', qblk, k).astype(jnp.float32) * scale
        p = jax.nn.softmax(s, axis=-1)
        a = (p[:, :, 0] - lam * p[:, :, 1]).astype(v.dtype)
        return jnp.einsum('bhqk,bkhe->bqhe', a, v)

    o = lax.map(block, qb)
    return jnp.moveaxis(o, 0, 1).reshape(b, t, N_HEADS_A, D_V_A)


def retention_chunkwise(q, k, v, log_g, state0):
    dt = q.dtype
    b, t, h, _ = q.shape
    dv = v.shape[-1]
    nc = t // CHUNK

    def chunks(a):
        return a.astype(jnp.float32).reshape(b, nc, CHUNK, h, a.shape[-1]).transpose(1, 0, 3, 2, 4)

    pos = jnp.arange(CHUNK, dtype=jnp.float32)
    rel = pos[:, None] - pos[None, :]
    decay_in = jnp.where(rel >= 0, jnp.exp(jnp.maximum(rel, 0.0) * log_g[:, None, None]), 0.0)
    xi = jnp.exp((pos + 1.0) * log_g[:, None])[:, :, None]
    zeta = jnp.exp((CHUNK - 1.0 - pos) * log_g[:, None])[:, :, None]
    chunk_decay = jnp.exp(CHUNK * log_g)[:, None, None]

    def step(state, inp):
        qc, kc, vc = inp
        s = jnp.einsum('bhqd,bhkd->bhqk', qc, kc) * decay_in
        o = jnp.einsum('bhqk,bhke->bhqe', s, vc) + jnp.einsum('bhqd,bhde->bhqe', qc, state) * xi
        state = state * chunk_decay + jnp.einsum('bhkd,bhke->bhde', kc * zeta, vc)
        return state, o

    state, o = lax.scan(step, state0.astype(jnp.float32), (chunks(q), chunks(k), chunks(v)))
    o = o.transpose(1, 0, 3, 2, 4).reshape(b, t, h, dv)
    return o.astype(dt), state.astype(dt)


def spatial_gate(u, v, gain, w_s, b_s):
    b, t, _ = v.shape
    nc = t // CHUNK
    u = jax.nn.gelu(u)
    vn = rmsnorm(jax.nn.gelu(v), gain).reshape(b, nc, CHUNK, N_GROUPS_B, D_GROUP_B)
    mixed = jnp.einsum('gpm,bnmgc->bnpgc', w_s, vn) + b_s.T[:, :, None]
    return u * mixed.reshape(b, t, W_B)


def hier_moe(h, lw):
    lg = (h @ lw['w_rg']).astype(jnp.float32) + lw['b_rg']
    p_group = jnp.max(jax.nn.softmax(lg, axis=-1), axis=-1, keepdims=True)
    gsel = jnp.argmax(lg, axis=-1)
    le = ((h @ lw['w_re']).astype(jnp.float32) + lw['b_re']).reshape(h.shape[:2] + (N_EXPERT_GROUPS, EXPERTS_PER_GROUP))
    le_sel = jnp.sum(le * jax.nn.one_hot(gsel, N_EXPERT_GROUPS, dtype=jnp.float32)[..., None], axis=-2)
    top_v, top_i = lax.top_k(le_sel, TOP_K_INNER)
    wts = p_group * jax.nn.softmax(top_v, axis=-1)
    expert_id = gsel[..., None] * EXPERTS_PER_GROUP + top_i
    dense_w = jnp.einsum('btk,btke->bte', wts, jax.nn.one_hot(expert_id, N_EXPERTS, dtype=jnp.float32))
    a = jnp.einsum('btd,edf->btef', h, lw['w_e_gate'])
    up = jnp.einsum('btd,edf->btef', h, lw['w_e_up'])
    act = jax.nn.silu(a) * up * dense_w[..., None].astype(h.dtype)
    return jnp.einsum('btef,efd->btd', act, lw['w_e_down'])


def trunk_layer(x, cond, lw, lam_init, rope, ctx):
    b, t, _ = x.shape
    mod = jax.nn.silu(cond) @ lw['w_mod'] + lw['b_mod']
    sh1, sc1, g1, sh2, sc2, g2 = jnp.split(mod[:, None, :], N_MOD, axis=-1)
    h = rmsnorm(x, lw['norm1']) * (1.0 + sc1) + sh1
    offsets = [int(o) for o in np.cumsum(IN_WIDTHS)[:-1]]
    qa, ka, va, ub, vb, qr, kr, vr, gr, ga, gb, gc = jnp.split(h @ lw['w_in'], offsets, axis=-1)
    qa = qa.reshape(b, t, N_HEADS_A, 2, D_HEAD_A)
    ka = ka.reshape(b, t, N_HEADS_A, 2, D_HEAD_A)
    va = va.reshape(b, t, N_HEADS_A, D_V_A)
    qr = qr.reshape(b, t, N_HEADS_R, D_K_R)
    kr = kr.reshape(b, t, N_HEADS_R, D_K_R) * (D_K_R ** -0.5)
    vr = vr.reshape(b, t, N_HEADS_R, D_V_R)
    if rope is not None:
        cos, sin = rope
        qa, ka, qr, kr = (apply_rope(a, cos, sin) for a in (qa, ka, qr, kr))
    if ctx is None:
        k_all, v_all = ka, va
        s_fwd = jnp.zeros((b, N_HEADS_R, D_K_R, D_V_R), jnp.float32)
        s_bwd = s_fwd
    else:
        ctx_k, ctx_v, ctx_s = ctx
        k_all = jnp.concatenate([ctx_k.astype(ka.dtype), ka], axis=1)
        v_all = jnp.concatenate([ctx_v.astype(va.dtype), va], axis=1)
        s_fwd, s_bwd = ctx_s[:, 0], ctx_s[:, 1]

    lp = lw['lambda_p'].astype(jnp.float32)
    lam = jnp.exp(jnp.sum(lp[0] * lp[1])) - jnp.exp(jnp.sum(lp[2] * lp[3])) + lam_init
    oa = rmsnorm(diff_attention(qa, k_all, v_all, lam), lw['diff_norm']) * (1.0 - lam_init)
    ya = oa.reshape(b, t, W_VA) @ lw['w_up_a']

    yb = spatial_gate(ub, vb, lw['sg_norm'], lw['sg_w'], lw['sg_b']) @ lw['w_up_b']

    log_g = jax.nn.log_sigmoid(lw['ret_decay'].astype(jnp.float32))
    o_f, r_f = retention_chunkwise(qr, kr, vr, log_g[0], s_fwd)
    o_b, r_b = retention_chunkwise(jnp.flip(qr, 1), jnp.flip(kr, 1), jnp.flip(vr, 1), log_g[1], s_bwd)
    orr = rmsnorm(o_f + jnp.flip(o_b, 1), lw['ret_norm'].reshape(N_HEADS_R, D_V_R))
    yc = (jax.nn.silu(gr) * orr.reshape(b, t, W_VR)) @ lw['w_up_c']

    merged = jax.nn.sigmoid(ga) * ya + jax.nn.sigmoid(gb) * yb + jax.nn.sigmoid(gc) * yc
    x = x + g1 * (merged @ lw['w_out'])
    h2 = rmsnorm(x, lw['norm2']) * (1.0 + sc2) + sh2
    x = x + g2 * hier_moe(h2, lw)
    if ctx is None:
        return x, (ka, va, jnp.stack([r_f, r_b], axis=1))
    return x, None


def setup_inputs(seed: int = 0) -> dict:
    key = jax.random.key(seed)
    ks = iter(jax.random.split(key, 40))
    f32 = jnp.float32
    D = D_MODEL

    def nrm(shape, scale):
        return jax.random.normal(next(ks), shape, f32) * scale

    p0 = 1.0 - 2.0 ** (-5.0 - jnp.arange(N_HEADS_R, dtype=f32))
    decay_logit = jnp.log(p0) - jnp.log1p(-p0)
    return {
        'x_prompt': nrm((BATCH, SEQ, D), 1.0),
        'x_sample': nrm((DEC_BATCH, DEC_SEQ, D), 1.0),
        'cache_k': nrm((DEC_BATCH, DEPTH, PAST_LEN, N_HEADS_A, 2, D_HEAD_A), 1.0),
        'cache_v': nrm((DEC_BATCH, DEPTH, PAST_LEN, N_HEADS_A, D_V_A), 1.0),
        'state_ret': nrm((DEC_BATCH, DEPTH, 2, N_HEADS_R, D_K_R, D_V_R), 2.0),
        'c': nrm((DEC_BATCH, D), 1.0),
        'c_ctx': nrm((D,), 1.0),
        'w_mod': nrm((DEPTH, D, N_MOD * D), D ** -0.5),
        'b_mod': nrm((DEPTH, N_MOD * D), 0.02),
        'norm1': 1.0 + nrm((DEPTH, D), 0.02),
        'w_in': nrm((DEPTH, D, D_IN), D ** -0.5),
        'lambda_p': nrm((DEPTH, 4, D_HEAD_A), 0.1),
        'diff_norm': 1.0 + nrm((DEPTH, D_V_A), 0.02),
        'sg_norm': 1.0 + nrm((DEPTH, W_B), 0.02),
        'sg_w': nrm((DEPTH, N_GROUPS_B, CHUNK, CHUNK), CHUNK ** -0.5),
        'sg_b': 1.0 + nrm((DEPTH, N_GROUPS_B, CHUNK), 0.1),
        'ret_decay': decay_logit[None, None, :] + nrm((DEPTH, 2, N_HEADS_R), 0.05),
        'ret_norm': 1.0 + nrm((DEPTH, W_VR), 0.02),
        'w_up_a': nrm((DEPTH, W_VA, D), W_VA ** -0.5),
        'w_up_b': nrm((DEPTH, W_B, D), W_B ** -0.5),
        'w_up_c': nrm((DEPTH, W_VR, D), W_VR ** -0.5),
        'w_out': nrm((DEPTH, D, D), D ** -0.5),
        'norm2': 1.0 + nrm((DEPTH, D), 0.02),
        'w_rg': nrm((DEPTH, D, N_EXPERT_GROUPS), D ** -0.5),
        'b_rg': nrm((DEPTH, N_EXPERT_GROUPS), 0.01),
        'w_re': nrm((DEPTH, D, N_EXPERTS), D ** -0.5),
        'b_re': nrm((DEPTH, N_EXPERTS), 0.01),
        'w_e_gate': nrm((DEPTH, N_EXPERTS, D, D_EXPERT), D ** -0.5),
        'w_e_up': nrm((DEPTH, N_EXPERTS, D, D_EXPERT), D ** -0.5),
        'w_e_down': nrm((DEPTH, N_EXPERTS, D_EXPERT, D), D_EXPERT ** -0.5),
        'final_norm': 1.0 + nrm((D,), 0.02),
    }


def reference(x_prompt, x_sample, cache_k, cache_v, state_ret, c, c_ctx, w_mod, b_mod, norm1, w_in,
              lambda_p, diff_norm, sg_norm, sg_w, sg_b, ret_decay, ret_norm, w_up_a, w_up_b, w_up_c,
              w_out, norm2, w_rg, b_rg, w_re, b_re, w_e_gate, w_e_up, w_e_down, final_norm):
    rope = axial_rope(x_sample.shape[1])
    cond_ctx = c_ctx[None, :]
    xp, xs = x_prompt, x_sample
    ks, vs, ss = [], [], []
    for l in range(DEPTH):
        lw = {'w_mod': w_mod[l], 'b_mod': b_mod[l], 'norm1': norm1[l], 'w_in': w_in[l],
              'lambda_p': lambda_p[l], 'diff_norm': diff_norm[l], 'sg_norm': sg_norm[l],
              'sg_w': sg_w[l], 'sg_b': sg_b[l], 'ret_decay': ret_decay[l], 'ret_norm': ret_norm[l],
              'w_up_a': w_up_a[l], 'w_up_b': w_up_b[l], 'w_up_c': w_up_c[l], 'w_out': w_out[l],
              'norm2': norm2[l], 'w_rg': w_rg[l], 'b_rg': b_rg[l], 'w_re': w_re[l], 'b_re': b_re[l],
              'w_e_gate': w_e_gate[l], 'w_e_up': w_e_up[l], 'w_e_down': w_e_down[l]}
        lam_init = 0.8 - 0.6 * math.exp(-0.3 * l)
        xp, (k_l, v_l, s_l) = trunk_layer(xp, cond_ctx, lw, lam_init, None, None)
        xs, _ = trunk_layer(xs, c, lw, lam_init, rope, (cache_k[:, l], cache_v[:, l], state_ret[:, l]))
        ks.append(k_l)
        vs.append(v_l)
        ss.append(s_l)
    y_prompt = rmsnorm(xp, final_norm)
    y_sample = rmsnorm(xs, final_norm)
    new_cache_k = jnp.stack(ks, axis=1)
    new_cache_v = jnp.stack(vs, axis=1)
    new_state_ret = jnp.stack(ss, axis=1)
    return (y_prompt, y_sample, new_cache_k, new_cache_v, new_state_ret)
```

```python
import functools
import math

import jax
import jax.numpy as jnp
from jax import lax
from jax.experimental import pallas as pl
from jax.experimental.pallas import tpu as pltpu

F32 = jnp.float32
BF16 = jnp.bfloat16

GRID_W = 64
CHUNK = 128
N_HEADS_A = 8
D_HEAD_A = 64
D_V_A = 128
N_GROUPS_B = 8
D_GROUP_B = 128
N_HEADS_R = 8
D_K_R = 64
D_V_R = 128
ROPE_DIM = 64
ROPE_BASE = 10000.0
N_EXPERT_GROUPS = 4
EXPERTS_PER_GROUP = 4
N_EXPERTS = 16
D_EXPERT = 256
N_MOD = 6
EPS = 1e-6

W_QA = N_HEADS_A * 2 * D_HEAD_A
W_VA = N_HEADS_A * D_V_A
W_B = N_GROUPS_B * D_GROUP_B
W_QR = N_HEADS_R * D_K_R
W_VR = N_HEADS_R * D_V_R

OFF_QA = 0
OFF_KA = OFF_QA + W_QA
OFF_VA = OFF_KA + W_QA
OFF_U = OFF_VA + W_VA
OFF_V = OFF_U + W_B
OFF_QR = OFF_V + W_B
OFF_KR = OFF_QR + W_QR
OFF_VR = OFF_KR + W_QR
OFF_GR = OFF_VR + W_VR
OFF_GA = OFF_GR + W_VR

LANES = 128
ROUTER_LANES = 128
VMEM_LIMIT = 56 * 1024 * 1024


def _params(n_axes, vmem=None):
    return pltpu.CompilerParams(dimension_semantics=("arbitrary",) * n_axes, vmem_limit_bytes=vmem)


def _sigmoid(x):
    return 1.0 / (1.0 + jnp.exp(-x))


def _silu(x):
    return x * _sigmoid(x)


def _gelu_tanh(x):
    return 0.5 * x * (1.0 + jnp.tanh(math.sqrt(2.0 / math.pi) * (x + 0.044715 * (x * x * x))))


def _rms(x):
    return x * lax.rsqrt(jnp.mean(x * x, axis=-1, keepdims=True) + EPS)


def _mod_body(c_ref, w_ref, b_ref, o_ref):
    a = _silu(c_ref[...]).astype(BF16)
    o_ref[...] = jnp.dot(a, w_ref[...].astype(BF16), preferred_element_type=F32) + b_ref[...]


def _mod_call(cond8, w_mod, b_mod):
    depth, d, n6 = w_mod.shape
    tn = 1024
    return pl.pallas_call(
        _mod_body,
        grid=(depth, n6 // tn),
        in_specs=[
            pl.BlockSpec((8, d), lambda l, j: (0, 0)),
            pl.BlockSpec((None, d, tn), lambda l, j: (l, 0, j)),
            pl.BlockSpec((None, 1, tn), lambda l, j: (l, 0, j)),
        ],
        out_specs=pl.BlockSpec((None, 8, tn), lambda l, j: (l, 0, j)),
        out_shape=jax.ShapeDtypeStruct((depth, 8, n6), F32),
        compiler_params=_params(2, VMEM_LIMIT),
        name="mod_vectors",
    )(cond8, w_mod, b_mod.reshape(depth, 1, n6))


def _norm_body(*refs, has_res, adaln):
    it = iter(refs)
    x_ref = next(it)
    if has_res:
        y_ref = next(it)
        modp_ref = next(it)
    g_ref = next(it)
    if adaln:
        mod_ref = next(it)
    if has_res and adaln:
        xo_ref = next(it)
    h_ref = next(it)

    x = x_ref[...]
    if has_res:
        x = x + modp_ref[5:6, :] * y_ref[...]
        if adaln:
            xo_ref[...] = x
    y = _rms(x) * g_ref[...]
    if adaln:
        y = y * (1.0 + mod_ref[1:2, :]) + mod_ref[0:1, :]
    h_ref[...] = y.astype(h_ref.dtype)


def _norm_call(x, gain, mod=None, res=None, *, t_batch, name):
    n, d = x.shape
    tm = min(512, t_batch)
    per = t_batch // tm
    adaln = mod is not None
    has_res = res is not None
    row = pl.BlockSpec((tm, d), lambda i: (i, 0))
    modspec = pl.BlockSpec((None, 8, d), lambda i: (i // per, 0, 0))
    args, specs = [x], [row]
    if has_res:
        args += [res[0], res[1]]
        specs += [row, modspec]
    args.append(gain.reshape(1, d))
    specs.append(pl.BlockSpec((1, d), lambda i: (0, 0)))
    if adaln:
        args.append(mod)
        specs.append(modspec)
    out_shape, out_specs = [], []
    if has_res and adaln:
        out_shape.append(jax.ShapeDtypeStruct((n, d), F32))
        out_specs.append(row)
    out_shape.append(jax.ShapeDtypeStruct((n, d), BF16 if adaln else F32))
    out_specs.append(row)
    outs = pl.pallas_call(
        functools.partial(_norm_body, has_res=has_res, adaln=adaln),
        grid=(n // tm,),
        in_specs=specs,
        out_specs=out_specs,
        out_shape=out_shape,
        compiler_params=_params(1, VMEM_LIMIT),
        name=name,
    )(*args)
    if has_res and adaln:
        return outs[0], outs[1]
    return (x, outs[0]) if adaln else outs[0]


def _rope_swap(blk):
    lane = lax.broadcasted_iota(jnp.int32, blk.shape, 1)
    return jnp.where((lane & 32) == 0, pltpu.roll(blk, LANES - 32, 1), pltpu.roll(blk, 32, 1))


def _inproj_body(*refs, rope, tn):
    if rope:
        h_ref, w_ref, cs_ref, sn_ref, o_ref, wbf_ref = refs
    else:
        h_ref, w_ref, o_ref, wbf_ref = refs
    n = pl.program_id(0)
    m = pl.program_id(1)

    @pl.when(m == 0)
    def _():
        wbf_ref[...] = w_ref[...].astype(BF16)

    o_ref[...] = jnp.dot(h_ref[...], wbf_ref[...], preferred_element_type=F32)

    def fix(j, scale):
        blk = o_ref[:, j * LANES:(j + 1) * LANES]
        if scale != 1.0:
            blk = blk * scale
        if rope:
            blk = blk * cs_ref[...] + _rope_swap(blk) * sn_ref[...]
        o_ref[:, j * LANES:(j + 1) * LANES] = blk

    groups = tn // LANES
    if rope:
        @pl.when(n < (OFF_VA // tn))
        def _():
            for j in range(groups):
                fix(j, 1.0)

    @pl.when(n == (OFF_QR // tn))
    def _():
        for j in range(groups):
            is_k = j * LANES >= W_QR
            if rope or is_k:
                fix(j, D_K_R ** -0.5 if is_k else 1.0)


def _inproj_call(h, w_in, layer, rope_tabs, *, t_batch, name):
    n_tok, d = h.shape
    d_in = w_in.shape[-1]
    tn = 1024
    assert OFF_QR % tn == 0 and OFF_QR + 2 * W_QR == OFF_QR + tn and d_in % tn == 0
    tm = min(1024, t_batch)
    per = t_batch // tm
    rope = rope_tabs is not None
    args = [h, w_in]
    specs = [
        pl.BlockSpec((tm, d), lambda n, m: (m, 0)),
        pl.BlockSpec((None, d, tn), lambda n, m: (layer, 0, n)),
    ]
    if rope:
        tab = pl.BlockSpec((tm, LANES), lambda n, m: (m % per, 0))
        args += list(rope_tabs)
        specs += [tab, tab]
    return pl.pallas_call(
        functools.partial(_inproj_body, rope=rope, tn=tn),
        grid=(d_in // tn, n_tok // tm),
        in_specs=specs,
        out_specs=pl.BlockSpec((tm, tn), lambda n, m: (m, n)),
        out_shape=jax.ShapeDtypeStruct((n_tok, d_in), F32),
        scratch_shapes=[pltpu.VMEM((d, tn), BF16)],
        compiler_params=_params(2, VMEM_LIMIT),
        name=name,
    )(*args)


def _attn_body(*refs, hb, t_new, past, lam_init):
    if past:
        lp_ref, q_ref, k_ref, v_ref, ck_ref, cv_ref, g_ref, o_ref, kbf, vbf = refs
    else:
        lp_ref, q_ref, k_ref, v_ref, g_ref, o_ref, kbf, vbf = refs
    i = pl.program_id(2)

    @pl.when(i == 0)
    def _():
        if past:
            kbf[0:past, :] = ck_ref[...].astype(BF16)
            vbf[0:past, :] = cv_ref[...].astype(BF16)
        kbf[past:past + t_new, :] = k_ref[...].astype(BF16)
        vbf[past:past + t_new, :] = v_ref[...].astype(BF16)

    lp = lp_ref[...]
    lam = (jnp.exp(jnp.sum(lp[0:1, :] * lp[1:2, :], keepdims=True))
           - jnp.exp(jnp.sum(lp[2:3, :] * lp[3:4, :], keepdims=True)) + lam_init)
    nt = (((1,), (1,)), ((), ()))
    for j in range(hb):
        sl = slice(j * LANES, (j + 1) * LANES)
        q = q_ref[:, sl] * (D_HEAD_A ** -0.5)
        lane = lax.broadcasted_iota(jnp.int32, q.shape, 1)
        q0 = jnp.where(lane < D_HEAD_A, q, 0.0).astype(BF16)
        q1 = jnp.where(lane >= D_HEAD_A, q, 0.0).astype(BF16)
        k = kbf[:, sl]
        s0 = lax.dot_general(q0, k, nt, preferred_element_type=F32)
        s1 = lax.dot_general(q1, k, nt, preferred_element_type=F32)
        p0 = jnp.exp(s0 - jnp.max(s0, axis=-1, keepdims=True))
        p1 = jnp.exp(s1 - jnp.max(s1, axis=-1, keepdims=True))
        r0 = 1.0 / jnp.sum(p0, axis=-1, keepdims=True)
        r1 = lam / jnp.sum(p1, axis=-1, keepdims=True)
        a = (p0 * r0 - p1 * r1).astype(BF16)
        o = jnp.dot(a, vbf[:, sl], preferred_element_type=F32)
        o_ref[:, sl] = ((_rms(o) * g_ref[...]) * (1.0 - lam_init)).astype(o_ref.dtype)


def _attn_call(proj, lambda_p, diff_norm, cache, layer, *, n_batch, t_batch, lam_init, hb, name):
    n_tok = proj.shape[0]
    tq = min(256, t_batch)
    qsteps = t_batch // tq
    w = hb * LANES
    past = 0 if cache is None else cache[0].shape[2]
    args = [lambda_p, proj, proj, proj]
    specs = [
        pl.BlockSpec((None, 4, D_HEAD_A), lambda b, h, i: (layer, 0, 0)),
        pl.BlockSpec((tq, w), lambda b, h, i: (b * qsteps + i, OFF_QA // w + h)),
        pl.BlockSpec((t_batch, w), lambda b, h, i: (b, OFF_KA // w + h)),
        pl.BlockSpec((t_batch, w), lambda b, h, i: (b, OFF_VA // w + h)),
    ]
    if past:
        cspec = pl.BlockSpec((None, None, past, w), lambda b, h, i: (b, layer, 0, h))
        args += [cache[0], cache[1]]
        specs += [cspec, cspec]
    args.append(diff_norm)
    specs.append(pl.BlockSpec((None, 1, D_V_A), lambda b, h, i: (layer, 0, 0)))
    return pl.pallas_call(
        functools.partial(_attn_body, hb=hb, t_new=t_batch, past=past, lam_init=lam_init),
        grid=(n_batch, N_HEADS_A // hb, qsteps),
        in_specs=specs,
        out_specs=pl.BlockSpec((tq, w), lambda b, h, i: (b * qsteps + i, h)),
        out_shape=jax.ShapeDtypeStruct((n_tok, W_VA), BF16),
        scratch_shapes=[pltpu.VMEM((past + t_batch, w), BF16), pltpu.VMEM((past + t_batch, w), BF16)],
        compiler_params=_params(3, VMEM_LIMIT),
        name=name,
    )(*args)


def _ret_body(*refs, hb, nc, has_state):
    if has_state:
        lg_ref, q_ref, k_ref, v_ref, s0_ref, of_ref, ob_ref, sf_ref, st_ref, tab_ref = refs
    else:
        lg_ref, q_ref, k_ref, v_ref, of_ref, ob_ref, sf_ref, st_ref, tab_ref = refs
    hblk = pl.program_id(1)
    c_ = CHUNK
    half = D_K_R

    row = lax.broadcasted_iota(jnp.int32, (c_, c_), 0).astype(F32)
    col = lax.broadcasted_iota(jnp.int32, (c_, c_), 1).astype(F32)
    rel = row - col
    for j in range(hb):
        for d in range(2):
            lg = lg_ref[d, hblk * hb + j]
            if d == 0:
                tab_ref[d, j, 0] = jnp.where(rel >= 0, jnp.exp(jnp.maximum(rel, 0.0) * lg), 0.0)
                tab_ref[d, j, 1] = jnp.exp((row + 1.0) * lg)
                tab_ref[d, j, 2] = jnp.exp((c_ - 1.0 - row) * lg)
            else:
                tab_ref[d, j, 0] = jnp.where(rel <= 0, jnp.exp(jnp.maximum(-rel, 0.0) * lg), 0.0)
                tab_ref[d, j, 1] = jnp.exp((c_ - row) * lg)
                tab_ref[d, j, 2] = jnp.exp(row * lg)
            tab_ref[d, j, 3] = jnp.exp(jnp.full((c_, c_), float(c_), F32) * lg)
            st_ref[d, j] = jnp.zeros((c_, D_V_R), F32)
            if has_state:
                lo = (j % 2) * half
                st_ref[d, j, lo:lo + half, :] = s0_ref[d, j]

    nt = (((1,), (1,)), ((), ()))

    def step(i, carry):
        for d in range(2):
            c = i if d == 0 else nc - 1 - i
            r0 = pl.multiple_of(c * c_, c_)
            for j in range(hb):
                jp = j // 2
                q = q_ref[pl.ds(r0, c_), jp * LANES:(jp + 1) * LANES]
                k = k_ref[pl.ds(r0, c_), jp * LANES:(jp + 1) * LANES]
                v = v_ref[pl.ds(r0, c_), j * D_V_R:(j + 1) * D_V_R].astype(BF16)
                lane = lax.broadcasted_iota(jnp.int32, q.shape, 1)
                mine = (lane >= (j % 2) * half) & (lane < (j % 2 + 1) * half)
                qm = jnp.where(mine, q, 0.0).astype(BF16)
                s = lax.dot_general(qm, k.astype(BF16), nt, preferred_element_type=F32) * tab_ref[d, j, 0]
                slab = st_ref[d, j]
                o = (jnp.dot(s.astype(BF16), v, preferred_element_type=F32)
                     + jnp.dot(qm, slab.astype(BF16), preferred_element_type=F32) * tab_ref[d, j, 1])
                kzt = (k * tab_ref[d, j, 2]).T.astype(BF16)
                st_ref[d, j] = slab * tab_ref[d, j, 3] + jnp.dot(kzt, v, preferred_element_type=F32)
                if d == 0:
                    of_ref[pl.ds(r0, c_), j * D_V_R:(j + 1) * D_V_R] = o
                else:
                    ob_ref[pl.ds(r0, c_), j * D_V_R:(j + 1) * D_V_R] = o
        return carry

    lax.fori_loop(0, nc, step, 0)
    for j in range(hb):
        lo = (j % 2) * half
        for d in range(2):
            sf_ref[d, j] = st_ref[d, j, lo:lo + half, :]


def _ret_call(proj, log_g, state, layer, *, n_batch, t_batch, hb, name):
    n_tok = proj.shape[0]
    nc = t_batch // CHUNK
    wq, wv = hb * D_K_R, hb * D_V_R
    args = [log_g, proj, proj, proj]
    specs = [
        pl.BlockSpec(memory_space=pltpu.SMEM),
        pl.BlockSpec((t_batch, wq), lambda b, h: (b, OFF_QR // wq + h)),
        pl.BlockSpec((t_batch, wq), lambda b, h: (b, OFF_KR // wq + h)),
        pl.BlockSpec((t_batch, wv), lambda b, h: (b, OFF_VR // wv + h)),
    ]
    has_state = state is not None
    if has_state:
        args.append(state)
        specs.append(pl.BlockSpec((None, None, 2, hb, D_K_R, D_V_R), lambda b, h: (b, layer, 0, h, 0, 0)))
    ospec = pl.BlockSpec((t_batch, wv), lambda b, h: (b, h))
    return pl.pallas_call(
        functools.partial(_ret_body, hb=hb, nc=nc, has_state=has_state),
        grid=(n_batch, N_HEADS_R // hb),
        in_specs=specs,
        out_specs=[ospec, ospec, pl.BlockSpec((None, 2, hb, D_K_R, D_V_R), lambda b, h: (b, 0, h, 0, 0))],
        out_shape=[
            jax.ShapeDtypeStruct((n_tok, W_VR), F32),
            jax.ShapeDtypeStruct((n_tok, W_VR), F32),
            jax.ShapeDtypeStruct((n_batch, 2, N_HEADS_R, D_K_R, D_V_R), F32),
        ],
        scratch_shapes=[pltpu.VMEM((2, hb, CHUNK, D_V_R), F32), pltpu.VMEM((2, hb, 4, CHUNK, CHUNK), F32)],
        compiler_params=_params(2, VMEM_LIMIT),
        name=name,
    )(*args)


def _mix_body(oa_ref, of_ref, ob_ref, u_ref, v_ref, gr_ref, ga_ref, gb_ref, gc_ref, sgn_ref, sgw_ref, sgb_ref,
              rn_ref, wa_ref, wb_ref, wc_ref, o_ref, sb_s, oc_s, *, tm):
    n = pl.program_id(1)

    @pl.when(n == 0)
    def _():
        for j in range(N_HEADS_R):
            sl = slice(j * D_V_R, (j + 1) * D_V_R)
            y = _rms(of_ref[:, sl] + ob_ref[:, sl]) * rn_ref[:, sl]
            oc_s[:, sl] = (_silu(gr_ref[:, sl]) * y).astype(BF16)
        gv = _gelu_tanh(v_ref[...])
        vn = (_rms(gv) * sgn_ref[...]).astype(BF16)
        for g in range(N_GROUPS_B):
            sl = slice(g * D_GROUP_B, (g + 1) * D_GROUP_B)
            wg = sgw_ref[g].astype(BF16)
            bias = sgb_ref[:, g:g + 1]
            for c in range(tm // CHUNK):
                rows = slice(c * CHUNK, (c + 1) * CHUNK)
                mixed = jnp.dot(wg, vn[rows, sl], preferred_element_type=F32) + bias
                sb_s[rows, sl] = (_gelu_tanh(u_ref[rows, sl]) * mixed).astype(BF16)

    ya = jnp.dot(oa_ref[...], wa_ref[...], preferred_element_type=F32)
    yb = jnp.dot(sb_s[...], wb_ref[...], preferred_element_type=F32)
    yc = jnp.dot(oc_s[...], wc_ref[...], preferred_element_type=F32)
    merged = _sigmoid(ga_ref[...]) * ya + _sigmoid(gb_ref[...]) * yb + _sigmoid(gc_ref[...]) * yc
    o_ref[...] = merged.astype(o_ref.dtype)


def _mix_call(oa, o_f, o_b, proj, sg_norm, sg_w, sg_bt, ret_norm, wa, wb, wc, layer, *, t_batch, name):
    n_tok = oa.shape[0]
    d = wa.shape[-1]
    tm = min(512, t_batch)
    tn = 512
    wide = pl.BlockSpec((tm, W_B), lambda m, n: (m, 0))

    def pcol(off, width):
        return pl.BlockSpec((tm, width), lambda m, n: (m, off // width))

    def gate(off):
        return pl.BlockSpec((tm, tn), lambda m, n: (m, off // tn + n))

    def lvec(width):
        return pl.BlockSpec((None, 1, width), lambda m, n: (layer, 0, 0))

    wspec = pl.BlockSpec((None, W_B, tn), lambda m, n: (layer, 0, n))
    return pl.pallas_call(
        functools.partial(_mix_body, tm=tm),
        grid=(n_tok // tm, d // tn),
        in_specs=[
            wide, wide, wide, pcol(OFF_U, W_B), pcol(OFF_V, W_B), pcol(OFF_GR, W_VR),
            gate(OFF_GA), gate(OFF_GA + d), gate(OFF_GA + 2 * d),
            lvec(W_B),
            pl.BlockSpec((None, N_GROUPS_B, CHUNK, CHUNK), lambda m, n: (layer, 0, 0, 0)),
            pl.BlockSpec((None, CHUNK, N_GROUPS_B), lambda m, n: (layer, 0, 0)),
            lvec(W_VR), wspec, wspec, wspec,
        ],
        out_specs=pl.BlockSpec((tm, tn), lambda m, n: (m, n)),
        out_shape=jax.ShapeDtypeStruct((n_tok, d), BF16),
        scratch_shapes=[pltpu.VMEM((tm, W_B), BF16), pltpu.VMEM((tm, W_VR), BF16)],
        compiler_params=_params(2, VMEM_LIMIT),
        name=name,
    )(oa, o_f, o_b, proj, proj, proj, proj, proj, proj, sg_norm, sg_w, sg_bt, ret_norm, wa, wb, wc)


def _route(logits):
    lane = lax.broadcasted_iota(jnp.int32, logits.shape, 1).astype(F32)
    big = float(1 << 20)
    neg = -jnp.inf
    lgm = jnp.where(lane < N_EXPERT_GROUPS, logits, neg)
    mx = jnp.max(lgm, axis=-1, keepdims=True)
    p_group = 1.0 / jnp.sum(jnp.exp(lgm - mx), axis=-1, keepdims=True)
    gsel = jnp.min(jnp.where(lgm == mx, lane, big), axis=-1, keepdims=True)
    lo = N_EXPERT_GROUPS + gsel * EXPERTS_PER_GROUP
    insel = (lane >= lo) & (lane < lo + EXPERTS_PER_GROUP)
    le = jnp.where(insel, logits, neg)
    v1 = jnp.max(le, axis=-1, keepdims=True)
    i1 = jnp.min(jnp.where(le == v1, lane, big), axis=-1, keepdims=True)
    le2 = jnp.where(lane == i1, neg, le)
    v2 = jnp.max(le2, axis=-1, keepdims=True)
    i2 = jnp.min(jnp.where(le2 == v2, lane, big), axis=-1, keepdims=True)
    e2 = jnp.exp(v2 - v1)
    den = 1.0 + e2
    w1 = p_group * (1.0 / den)
    w2 = p_group * (e2 / den)
    return jnp.where(lane == i1, w1, 0.0) + jnp.where(lane == i2, w2, 0.0)


def _outproj_body(mg_ref, x_ref, w_ref, mod_ref, g_ref, wr_ref, br_ref, x1_ref, h2_ref, dw_ref):
    y = jnp.dot(mg_ref[...], w_ref[...], preferred_element_type=F32)
    x1 = x_ref[...] + mod_ref[2:3, :] * y
    x1_ref[...] = x1
    h2 = ((_rms(x1) * g_ref[...]) * (1.0 + mod_ref[4:5, :]) + mod_ref[3:4, :]).astype(BF16)
    h2_ref[...] = h2
    logits = jnp.dot(h2, wr_ref[...], preferred_element_type=F32) + br_ref[...]
    dw_ref[...] = _route(logits)


def _outproj_call(merged, x, w_out, mod, norm2, w_router, b_router, layer, *, t_batch, name):
    n_tok, d = x.shape
    tm = min(256, t_batch)
    per = t_batch // tm
    row = pl.BlockSpec((tm, d), lambda i: (i, 0))
    return pl.pallas_call(
        _outproj_body,
        grid=(n_tok // tm,),
        in_specs=[
            row, row,
            pl.BlockSpec((None, d, d), lambda i: (layer, 0, 0)),
            pl.BlockSpec((None, 8, d), lambda i: (i // per, 0, 0)),
            pl.BlockSpec((None, 1, d), lambda i: (layer, 0, 0)),
            pl.BlockSpec((None, d, ROUTER_LANES), lambda i: (layer, 0, 0)),
            pl.BlockSpec((None, 1, ROUTER_LANES), lambda i: (layer, 0, 0)),
        ],
        out_specs=[row, row, pl.BlockSpec((tm, ROUTER_LANES), lambda i: (i, 0))],
        out_shape=[
            jax.ShapeDtypeStruct((n_tok, d), F32),
            jax.ShapeDtypeStruct((n_tok, d), BF16),
            jax.ShapeDtypeStruct((n_tok, ROUTER_LANES), F32),
        ],
        compiler_params=_params(1, VMEM_LIMIT),
        name=name,
    )(merged, x, w_out, mod, norm2, w_router, b_router)


def _moe_body(h_ref, dw_ref, wg_ref, wu_ref, wd_ref, o_ref):
    e = pl.program_id(1)

    @pl.when(e == 0)
    def _():
        o_ref[...] = jnp.zeros(o_ref.shape, F32)

    h = h_ref[...]
    a = jnp.dot(h, wg_ref[...].astype(BF16), preferred_element_type=F32)
    up = jnp.dot(h, wu_ref[...].astype(BF16), preferred_element_type=F32)
    lane = lax.broadcasted_iota(jnp.int32, dw_ref.shape, 1)
    w = jnp.sum(jnp.where(lane == N_EXPERT_GROUPS + e, dw_ref[...], 0.0), axis=-1, keepdims=True)
    act = ((_silu(a) * up) * w).astype(BF16)
    o_ref[...] += jnp.dot(act, wd_ref[...].astype(BF16), preferred_element_type=F32)


def _moe_call(h2, dw, w_gate, w_up, w_down, layer, *, name):
    n_tok, d = h2.shape
    tm = min(1024, n_tok)
    f = w_gate.shape[-1]
    return pl.pallas_call(
        _moe_body,
        grid=(n_tok // tm, N_EXPERTS),
        in_specs=[
            pl.BlockSpec((tm, d), lambda m, e: (m, 0)),
            pl.BlockSpec((tm, ROUTER_LANES), lambda m, e: (m, 0)),
            pl.BlockSpec((None, None, d, f), lambda m, e: (layer, e, 0, 0)),
            pl.BlockSpec((None, None, d, f), lambda m, e: (layer, e, 0, 0)),
            pl.BlockSpec((None, None, f, d), lambda m, e: (layer, e, 0, 0)),
        ],
        out_specs=pl.BlockSpec((tm, d), lambda m, e: (m, 0)),
        out_shape=jax.ShapeDtypeStruct((n_tok, d), F32),
        compiler_params=_params(2, VMEM_LIMIT),
        name=name,
    )(h2, dw, w_gate, w_up, w_down)


def _rope_tables(n_tok):
    rows = n_tok // GRID_W
    row = jnp.repeat(jnp.arange(rows, dtype=F32), GRID_W)
    col = jnp.tile(jnp.arange(GRID_W, dtype=F32), rows)
    n_freq = ROPE_DIM // 4
    inv = ROPE_BASE ** (-jnp.arange(n_freq, dtype=F32) / n_freq)
    ang = jnp.concatenate([row[:, None] * inv, col[:, None] * inv], axis=-1)
    cos, sin = jnp.cos(ang), jnp.sin(ang)
    return jnp.concatenate([cos, cos, cos, cos], axis=-1), jnp.concatenate([-sin, sin, -sin, sin], axis=-1)


def kernel(x_prompt, x_sample, cache_k, cache_v, state_ret, c, c_ctx, w_mod, b_mod, norm1, w_in, lambda_p, diff_norm, sg_norm, sg_w, sg_b, ret_decay, ret_norm, w_up_a, w_up_b, w_up_c, w_out, norm2, w_rg, b_rg, w_re, b_re, w_e_gate, w_e_up, w_e_down, final_norm):
    depth = w_in.shape[0]
    bp, tp, d = x_prompt.shape
    bs, ts, _ = x_sample.shape
    past = cache_k.shape[2]

    cond8 = jnp.zeros((8, d), F32).at[0].set(c_ctx).at[1:1 + bs].set(c)
    mods = _mod_call(cond8, w_mod, b_mod).reshape(depth, 8, N_MOD, d)
    mods = jnp.pad(mods, ((0, 0), (0, 0), (0, 8 - N_MOD), (0, 0)))

    wa_b, wb_b, wc_b, wo_b = (w.astype(BF16) for w in (w_up_a, w_up_b, w_up_c, w_out))
    w_router = jnp.concatenate(
        [w_rg, w_re, jnp.zeros((depth, d, ROUTER_LANES - N_EXPERT_GROUPS - N_EXPERTS), F32)], axis=-1).astype(BF16)
    b_router = jnp.concatenate(
        [b_rg, b_re, jnp.zeros((depth, ROUTER_LANES - N_EXPERT_GROUPS - N_EXPERTS), F32)], axis=-1)[:, None, :]
    log_g = jax.nn.log_sigmoid(ret_decay.astype(F32))
    sg_bt = jnp.swapaxes(sg_b, 1, 2)
    rope_tabs = _rope_tables(ts)
    cache_k2 = cache_k.reshape(bs, depth, past, W_QA)
    cache_v2 = cache_v.reshape(bs, depth, past, W_VA)

    groups = [
        dict(tag="p", x=x_prompt.reshape(bp * tp, d), nb=bp, t=tp, tt=bp * tp, rope=None, ctx=False, hb_a=8, hb_r=8),
        dict(tag="s", x=x_sample.reshape(bs * ts, d), nb=bs, t=ts, tt=ts, rope=rope_tabs, ctx=True, hb_a=1, hb_r=2),
    ]
    ks, vs, ss = [], [], []
    for g in groups:
        g["res"] = None
    for l in range(depth):
        lam_init = 0.8 - 0.6 * math.exp(-0.3 * l)
        for g in groups:
            tag, nb, t, tt = g["tag"], g["nb"], g["t"], g["tt"]
            mod = mods[l, 0:1] if not g["ctx"] else mods[l, 1:1 + nb]
            x, h = _norm_call(g["x"], norm1[l], mod, g["res"], t_batch=tt, name=f"norm1_{tag}{l}")
            proj = _inproj_call(h, w_in, l, g["rope"], t_batch=tt, name=f"inproj_{tag}{l}")
            cache = (cache_k2, cache_v2) if g["ctx"] else None
            oa = _attn_call(proj, lambda_p, diff_norm[:, None, :], cache, l, n_batch=nb, t_batch=t,
                            lam_init=lam_init, hb=g["hb_a"], name=f"attn_{tag}{l}")
            o_f, o_b, s_fin = _ret_call(proj, log_g[l], state_ret if g["ctx"] else None, l, n_batch=nb, t_batch=t,
                                        hb=g["hb_r"], name=f"ret_{tag}{l}")
            merged = _mix_call(oa, o_f, o_b, proj, sg_norm[:, None, :], sg_w, sg_bt, ret_norm[:, None, :],
                               wa_b, wb_b, wc_b, l, t_batch=tt, name=f"mix_{tag}{l}")
            x1, h2, dw = _outproj_call(merged, x, wo_b, mod, norm2[:, None, :], w_router, b_router, l,
                                       t_batch=tt, name=f"outproj_{tag}{l}")
            y = _moe_call(h2, dw, w_e_gate, w_e_up, w_e_down, l, name=f"moe_{tag}{l}")
            g["x"], g["res"] = x1, (y, mod)
            if not g["ctx"]:
                ks.append(proj[:, OFF_KA:OFF_KA + W_QA].reshape(nb, t, N_HEADS_A, 2, D_HEAD_A))
                vs.append(proj[:, OFF_VA:OFF_VA + W_VA].reshape(nb, t, N_HEADS_A, D_V_A))
                ss.append(s_fin)
    outs = []
    for g in groups:
        yn = _norm_call(g["x"], final_norm, None, g["res"], t_batch=g["tt"], name=f"final_{g['tag']}")
        outs.append(yn.reshape(g["nb"], g["t"], d))
    return (outs[0], outs[1], jnp.stack(ks, axis=1), jnp.stack(vs, axis=1), jnp.stack(ss, axis=1))
```

```python
import functools
import math

import jax
import jax.numpy as jnp
from jax import lax
from jax.experimental import pallas as pl
from jax.experimental.pallas import tpu as pltpu

F32 = jnp.float32
BF16 = jnp.bfloat16

GRID_W = 64
CHUNK = 128
N_HEADS_A = 8
D_HEAD_A = 64
D_V_A = 128
N_GROUPS_B = 8
D_GROUP_B = 128
N_HEADS_R = 8
D_K_R = 64
D_V_R = 128
ROPE_DIM = 64
ROPE_BASE = 10000.0
N_EXPERT_GROUPS = 4
EXPERTS_PER_GROUP = 4
N_EXPERTS = 16
D_EXPERT = 256
N_MOD = 6
EPS = 1e-6

W_QA = N_HEADS_A * 2 * D_HEAD_A
W_VA = N_HEADS_A * D_V_A
W_B = N_GROUPS_B * D_GROUP_B
W_QR = N_HEADS_R * D_K_R
W_VR = N_HEADS_R * D_V_R

OFF_QA = 0
OFF_KA = OFF_QA + W_QA
OFF_VA = OFF_KA + W_QA
OFF_U = OFF_VA + W_VA
OFF_V = OFF_U + W_B
OFF_QR = OFF_V + W_B
OFF_KR = OFF_QR + W_QR
OFF_VR = OFF_KR + W_QR
OFF_GR = OFF_VR + W_VR
OFF_GA = OFF_GR + W_VR

LANES = 128
ROUTER_LANES = 128
VMEM_LIMIT = 56 * 1024 * 1024


def _params(n_axes, vmem=None):
    return pltpu.CompilerParams(dimension_semantics=("arbitrary",) * n_axes, vmem_limit_bytes=vmem)


def _sigmoid(x):
    return 1.0 / (1.0 + jnp.exp(-x))


def _silu(x):
    return x * _sigmoid(x)


def _gelu_tanh(x):
    return 0.5 * x * (1.0 + jnp.tanh(math.sqrt(2.0 / math.pi) * (x + 0.044715 * (x * x * x))))


def _rms(x):
    return x * lax.rsqrt(jnp.mean(x * x, axis=-1, keepdims=True) + EPS)


def _mod_body(c_ref, w_ref, b_ref, o_ref):
    a = _silu(c_ref[...]).astype(BF16)
    o_ref[...] = jnp.dot(a, w_ref[...].astype(BF16), preferred_element_type=F32) + b_ref[...]


def _mod_call(cond8, w_mod, b_mod):
    depth, d, n6 = w_mod.shape
    tn = 1024
    return pl.pallas_call(
        _mod_body,
        grid=(depth, n6 // tn),
        in_specs=[
            pl.BlockSpec((8, d), lambda l, j: (0, 0)),
            pl.BlockSpec((None, d, tn), lambda l, j: (l, 0, j)),
            pl.BlockSpec((None, 1, tn), lambda l, j: (l, 0, j)),
        ],
        out_specs=pl.BlockSpec((None, 8, tn), lambda l, j: (l, 0, j)),
        out_shape=jax.ShapeDtypeStruct((depth, 8, n6), F32),
        compiler_params=_params(2, VMEM_LIMIT),
        name="mod_vectors",
    )(cond8, w_mod, b_mod.reshape(depth, 1, n6))


def _norm_body(*refs, has_res, adaln):
    it = iter(refs)
    x_ref = next(it)
    if has_res:
        y_ref = next(it)
        modp_ref = next(it)
    g_ref = next(it)
    if adaln:
        mod_ref = next(it)
    if has_res and adaln:
        xo_ref = next(it)
    h_ref = next(it)

    x = x_ref[...]
    if has_res:
        x = x + modp_ref[5:6, :] * y_ref[...]
        if adaln:
            xo_ref[...] = x
    y = _rms(x) * g_ref[...]
    if adaln:
        y = y * (1.0 + mod_ref[1:2, :]) + mod_ref[0:1, :]
    h_ref[...] = y.astype(h_ref.dtype)


def _norm_call(x, gain, mod=None, res=None, *, t_batch, name):
    n, d = x.shape
    tm = min(512, t_batch)
    per = t_batch // tm
    adaln = mod is not None
    has_res = res is not None
    row = pl.BlockSpec((tm, d), lambda i: (i, 0))
    modspec = pl.BlockSpec((None, 8, d), lambda i: (i // per, 0, 0))
    args, specs = [x], [row]
    if has_res:
        y_all, row0, mod_prev = res
        assert row0 % tm == 0
        args += [y_all, mod_prev]
        specs += [pl.BlockSpec((tm, d), lambda i: (i + row0 // tm, 0)), modspec]
    args.append(gain.reshape(1, d))
    specs.append(pl.BlockSpec((1, d), lambda i: (0, 0)))
    if adaln:
        args.append(mod)
        specs.append(modspec)
    out_shape, out_specs = [], []
    if has_res and adaln:
        out_shape.append(jax.ShapeDtypeStruct((n, d), F32))
        out_specs.append(row)
    out_shape.append(jax.ShapeDtypeStruct((n, d), BF16 if adaln else F32))
    out_specs.append(row)
    outs = pl.pallas_call(
        functools.partial(_norm_body, has_res=has_res, adaln=adaln),
        grid=(n // tm,),
        in_specs=specs,
        out_specs=out_specs,
        out_shape=out_shape,
        compiler_params=_params(1, VMEM_LIMIT),
        name=name,
    )(*args)
    if has_res and adaln:
        return outs[0], outs[1]
    return (x, outs[0]) if adaln else outs[0]


def _rope_swap(blk):
    lane = lax.broadcasted_iota(jnp.int32, blk.shape, 1)
    return jnp.where((lane & 32) == 0, pltpu.roll(blk, LANES - 32, 1), pltpu.roll(blk, 32, 1))


def _inproj_body(*refs, rope, tn):
    if rope:
        h_ref, w_ref, cs_ref, sn_ref, o_ref, wbf_ref = refs
    else:
        h_ref, w_ref, o_ref, wbf_ref = refs
    n = pl.program_id(0)
    m = pl.program_id(1)

    @pl.when(m == 0)
    def _():
        wbf_ref[...] = w_ref[...].astype(BF16)

    o_ref[...] = jnp.dot(h_ref[...], wbf_ref[...], preferred_element_type=F32)

    def fix(j, scale):
        blk = o_ref[:, j * LANES:(j + 1) * LANES]
        if scale != 1.0:
            blk = blk * scale
        if rope:
            blk = blk * cs_ref[...] + _rope_swap(blk) * sn_ref[...]
        o_ref[:, j * LANES:(j + 1) * LANES] = blk

    groups = tn // LANES
    if rope:
        @pl.when(n < (OFF_VA // tn))
        def _():
            for j in range(groups):
                fix(j, 1.0)

    @pl.when(n == (OFF_QR // tn))
    def _():
        for j in range(groups):
            is_k = j * LANES >= W_QR
            if rope or is_k:
                fix(j, D_K_R ** -0.5 if is_k else 1.0)


def _inproj_call(h, w_in, layer, rope_tabs, *, t_batch, name):
    n_tok, d = h.shape
    d_in = w_in.shape[-1]
    tn = 1024
    assert OFF_QR % tn == 0 and OFF_QR + 2 * W_QR == OFF_QR + tn and d_in % tn == 0
    tm = min(1024, t_batch)
    per = t_batch // tm
    rope = rope_tabs is not None
    args = [h, w_in]
    specs = [
        pl.BlockSpec((tm, d), lambda n, m: (m, 0)),
        pl.BlockSpec((None, d, tn), lambda n, m: (layer, 0, n)),
    ]
    if rope:
        tab = pl.BlockSpec((tm, LANES), lambda n, m: (m % per, 0))
        args += list(rope_tabs)
        specs += [tab, tab]
    return pl.pallas_call(
        functools.partial(_inproj_body, rope=rope, tn=tn),
        grid=(d_in // tn, n_tok // tm),
        in_specs=specs,
        out_specs=pl.BlockSpec((tm, tn), lambda n, m: (m, n)),
        out_shape=jax.ShapeDtypeStruct((n_tok, d_in), F32),
        scratch_shapes=[pltpu.VMEM((d, tn), BF16)],
        compiler_params=_params(2, VMEM_LIMIT),
        name=name,
    )(*args)


def _attn_body(*refs, hb, t_new, past, lam_init):
    if past:
        lp_ref, q_ref, k_ref, v_ref, ck_ref, cv_ref, g_ref, o_ref, kbf, vbf = refs
    else:
        lp_ref, q_ref, k_ref, v_ref, g_ref, o_ref, kbf, vbf = refs
    i = pl.program_id(2)

    @pl.when(i == 0)
    def _():
        if past:
            kbf[0:past, :] = ck_ref[...].astype(BF16)
            vbf[0:past, :] = cv_ref[...].astype(BF16)
        kbf[past:past + t_new, :] = k_ref[...].astype(BF16)
        vbf[past:past + t_new, :] = v_ref[...].astype(BF16)

    lp = lp_ref[...]
    lam = (jnp.exp(jnp.sum(lp[0:1, :] * lp[1:2, :], keepdims=True))
           - jnp.exp(jnp.sum(lp[2:3, :] * lp[3:4, :], keepdims=True)) + lam_init)
    nt = (((1,), (1,)), ((), ()))
    for j in range(hb):
        sl = slice(j * LANES, (j + 1) * LANES)
        q = q_ref[:, sl] * (D_HEAD_A ** -0.5)
        lane = lax.broadcasted_iota(jnp.int32, q.shape, 1)
        q0 = jnp.where(lane < D_HEAD_A, q, 0.0).astype(BF16)
        q1 = jnp.where(lane >= D_HEAD_A, q, 0.0).astype(BF16)
        k = kbf[:, sl]
        s0 = lax.dot_general(q0, k, nt, preferred_element_type=F32)
        s1 = lax.dot_general(q1, k, nt, preferred_element_type=F32)
        p0 = jnp.exp(s0 - jnp.max(s0, axis=-1, keepdims=True))
        p1 = jnp.exp(s1 - jnp.max(s1, axis=-1, keepdims=True))
        r0 = 1.0 / jnp.sum(p0, axis=-1, keepdims=True)
        r1 = lam / jnp.sum(p1, axis=-1, keepdims=True)
        a = (p0 * r0 - p1 * r1).astype(BF16)
        o = jnp.dot(a, vbf[:, sl], preferred_element_type=F32)
        o_ref[:, sl] = ((_rms(o) * g_ref[...]) * (1.0 - lam_init)).astype(o_ref.dtype)


def _attn_call(proj, lambda_p, diff_norm, cache, layer, *, n_batch, t_batch, lam_init, hb, name):
    n_tok = proj.shape[0]
    tq = min(256, t_batch)
    qsteps = t_batch // tq
    w = hb * LANES
    past = 0 if cache is None else cache[0].shape[2]
    args = [lambda_p, proj, proj, proj]
    specs = [
        pl.BlockSpec((None, 4, D_HEAD_A), lambda b, h, i: (layer, 0, 0)),
        pl.BlockSpec((tq, w), lambda b, h, i: (b * qsteps + i, OFF_QA // w + h)),
        pl.BlockSpec((t_batch, w), lambda b, h, i: (b, OFF_KA // w + h)),
        pl.BlockSpec((t_batch, w), lambda b, h, i: (b, OFF_VA // w + h)),
    ]
    if past:
        cspec = pl.BlockSpec((None, None, past, w), lambda b, h, i: (b, layer, 0, h))
        args += [cache[0], cache[1]]
        specs += [cspec, cspec]
    args.append(diff_norm)
    specs.append(pl.BlockSpec((None, 1, D_V_A), lambda b, h, i: (layer, 0, 0)))
    return pl.pallas_call(
        functools.partial(_attn_body, hb=hb, t_new=t_batch, past=past, lam_init=lam_init),
        grid=(n_batch, N_HEADS_A // hb, qsteps),
        in_specs=specs,
        out_specs=pl.BlockSpec((tq, w), lambda b, h, i: (b * qsteps + i, h)),
        out_shape=jax.ShapeDtypeStruct((n_tok, W_VA), BF16),
        scratch_shapes=[pltpu.VMEM((past + t_batch, w), BF16), pltpu.VMEM((past + t_batch, w), BF16)],
        compiler_params=_params(3, VMEM_LIMIT),
        name=name,
    )(*args)


def _ret_body(*refs, hb, nc, has_state):
    if has_state:
        lg_ref, q_ref, k_ref, v_ref, s0_ref, of_ref, ob_ref, sf_ref, st_ref, tab_ref = refs
    else:
        lg_ref, q_ref, k_ref, v_ref, of_ref, ob_ref, sf_ref, st_ref, tab_ref = refs
    hblk = pl.program_id(1)
    c_ = CHUNK
    half = D_K_R

    row = lax.broadcasted_iota(jnp.int32, (c_, c_), 0).astype(F32)
    col = lax.broadcasted_iota(jnp.int32, (c_, c_), 1).astype(F32)
    rel = row - col
    for j in range(hb):
        for d in range(2):
            lg = lg_ref[d, hblk * hb + j]
            if d == 0:
                tab_ref[d, j, 0] = jnp.where(rel >= 0, jnp.exp(jnp.maximum(rel, 0.0) * lg), 0.0)
                tab_ref[d, j, 1] = jnp.exp((row + 1.0) * lg)
                tab_ref[d, j, 2] = jnp.exp((c_ - 1.0 - row) * lg)
            else:
                tab_ref[d, j, 0] = jnp.where(rel <= 0, jnp.exp(jnp.maximum(-rel, 0.0) * lg), 0.0)
                tab_ref[d, j, 1] = jnp.exp((c_ - row) * lg)
                tab_ref[d, j, 2] = jnp.exp(row * lg)
            tab_ref[d, j, 3] = jnp.exp(jnp.full((c_, c_), float(c_), F32) * lg)
            st_ref[d, j] = jnp.zeros((c_, D_V_R), F32)
            if has_state:
                lo = (j % 2) * half
                st_ref[d, j, lo:lo + half, :] = s0_ref[d, j]

    nt = (((1,), (1,)), ((), ()))

    def step(i, carry):
        for d in range(2):
            c = i if d == 0 else nc - 1 - i
            r0 = pl.multiple_of(c * c_, c_)
            for j in range(hb):
                jp = j // 2
                q = q_ref[pl.ds(r0, c_), jp * LANES:(jp + 1) * LANES]
                k = k_ref[pl.ds(r0, c_), jp * LANES:(jp + 1) * LANES]
                v = v_ref[pl.ds(r0, c_), j * D_V_R:(j + 1) * D_V_R].astype(BF16)
                lane = lax.broadcasted_iota(jnp.int32, q.shape, 1)
                mine = (lane >= (j % 2) * half) & (lane < (j % 2 + 1) * half)
                qm = jnp.where(mine, q, 0.0).astype(BF16)
                s = lax.dot_general(qm, k.astype(BF16), nt, preferred_element_type=F32) * tab_ref[d, j, 0]
                slab = st_ref[d, j]
                o = (jnp.dot(s.astype(BF16), v, preferred_element_type=F32)
                     + jnp.dot(qm, slab.astype(BF16), preferred_element_type=F32) * tab_ref[d, j, 1])
                kzt = (k * tab_ref[d, j, 2]).T.astype(BF16)
                st_ref[d, j] = slab * tab_ref[d, j, 3] + jnp.dot(kzt, v, preferred_element_type=F32)
                if d == 0:
                    of_ref[pl.ds(r0, c_), j * D_V_R:(j + 1) * D_V_R] = o
                else:
                    ob_ref[pl.ds(r0, c_), j * D_V_R:(j + 1) * D_V_R] = o
        return carry

    lax.fori_loop(0, nc, step, 0)
    for j in range(hb):
        lo = (j % 2) * half
        for d in range(2):
            sf_ref[d, j] = st_ref[d, j, lo:lo + half, :]


def _ret_call(proj, log_g, state, layer, *, n_batch, t_batch, hb, name):
    n_tok = proj.shape[0]
    nc = t_batch // CHUNK
    wq, wv = hb * D_K_R, hb * D_V_R
    args = [log_g, proj, proj, proj]
    specs = [
        pl.BlockSpec(memory_space=pltpu.SMEM),
        pl.BlockSpec((t_batch, wq), lambda b, h: (b, OFF_QR // wq + h)),
        pl.BlockSpec((t_batch, wq), lambda b, h: (b, OFF_KR // wq + h)),
        pl.BlockSpec((t_batch, wv), lambda b, h: (b, OFF_VR // wv + h)),
    ]
    has_state = state is not None
    if has_state:
        args.append(state)
        specs.append(pl.BlockSpec((None, None, 2, hb, D_K_R, D_V_R), lambda b, h: (b, layer, 0, h, 0, 0)))
    ospec = pl.BlockSpec((t_batch, wv), lambda b, h: (b, h))
    return pl.pallas_call(
        functools.partial(_ret_body, hb=hb, nc=nc, has_state=has_state),
        grid=(n_batch, N_HEADS_R // hb),
        in_specs=specs,
        out_specs=[ospec, ospec, pl.BlockSpec((None, 2, hb, D_K_R, D_V_R), lambda b, h: (b, 0, h, 0, 0))],
        out_shape=[
            jax.ShapeDtypeStruct((n_tok, W_VR), F32),
            jax.ShapeDtypeStruct((n_tok, W_VR), F32),
            jax.ShapeDtypeStruct((n_batch, 2, N_HEADS_R, D_K_R, D_V_R), F32),
        ],
        scratch_shapes=[pltpu.VMEM((2, hb, CHUNK, D_V_R), F32), pltpu.VMEM((2, hb, 4, CHUNK, CHUNK), F32)],
        compiler_params=_params(2, VMEM_LIMIT),
        name=name,
    )(*args)


def _mix_body(oa_ref, of_ref, ob_ref, u_ref, v_ref, gr_ref, ga_ref, gb_ref, gc_ref, sgn_ref, sgw_ref, sgb_ref,
              rn_ref, wa_ref, wb_ref, wc_ref, o_ref, sb_s, oc_s, *, tm):
    n = pl.program_id(1)

    @pl.when(n == 0)
    def _():
        for j in range(N_HEADS_R):
            sl = slice(j * D_V_R, (j + 1) * D_V_R)
            y = _rms(of_ref[:, sl] + ob_ref[:, sl]) * rn_ref[:, sl]
            oc_s[:, sl] = (_silu(gr_ref[:, sl]) * y).astype(BF16)
        gv = _gelu_tanh(v_ref[...])
        vn = (_rms(gv) * sgn_ref[...]).astype(BF16)
        for g in range(N_GROUPS_B):
            sl = slice(g * D_GROUP_B, (g + 1) * D_GROUP_B)
            wg = sgw_ref[g].astype(BF16)
            bias = sgb_ref[:, g:g + 1]
            for c in range(tm // CHUNK):
                rows = slice(c * CHUNK, (c + 1) * CHUNK)
                mixed = jnp.dot(wg, vn[rows, sl], preferred_element_type=F32) + bias
                sb_s[rows, sl] = (_gelu_tanh(u_ref[rows, sl]) * mixed).astype(BF16)

    ya = jnp.dot(oa_ref[...], wa_ref[...], preferred_element_type=F32)
    yb = jnp.dot(sb_s[...], wb_ref[...], preferred_element_type=F32)
    yc = jnp.dot(oc_s[...], wc_ref[...], preferred_element_type=F32)
    merged = _sigmoid(ga_ref[...]) * ya + _sigmoid(gb_ref[...]) * yb + _sigmoid(gc_ref[...]) * yc
    o_ref[...] = merged.astype(o_ref.dtype)


def _mix_call(oa, o_f, o_b, proj, sg_norm, sg_w, sg_bt, ret_norm, wa, wb, wc, layer, *, t_batch, name):
    n_tok = oa.shape[0]
    d = wa.shape[-1]
    tm = min(512, t_batch)
    tn = 512
    wide = pl.BlockSpec((tm, W_B), lambda m, n: (m, 0))

    def pcol(off, width):
        return pl.BlockSpec((tm, width), lambda m, n: (m, off // width))

    def gate(off):
        return pl.BlockSpec((tm, tn), lambda m, n: (m, off // tn + n))

    def lvec(width):
        return pl.BlockSpec((None, 1, width), lambda m, n: (layer, 0, 0))

    wspec = pl.BlockSpec((None, W_B, tn), lambda m, n: (layer, 0, n))
    return pl.pallas_call(
        functools.partial(_mix_body, tm=tm),
        grid=(n_tok // tm, d // tn),
        in_specs=[
            wide, wide, wide, pcol(OFF_U, W_B), pcol(OFF_V, W_B), pcol(OFF_GR, W_VR),
            gate(OFF_GA), gate(OFF_GA + d), gate(OFF_GA + 2 * d),
            lvec(W_B),
            pl.BlockSpec((None, N_GROUPS_B, CHUNK, CHUNK), lambda m, n: (layer, 0, 0, 0)),
            pl.BlockSpec((None, CHUNK, N_GROUPS_B), lambda m, n: (layer, 0, 0)),
            lvec(W_VR), wspec, wspec, wspec,
        ],
        out_specs=pl.BlockSpec((tm, tn), lambda m, n: (m, n)),
        out_shape=jax.ShapeDtypeStruct((n_tok, d), BF16),
        scratch_shapes=[pltpu.VMEM((tm, W_B), BF16), pltpu.VMEM((tm, W_VR), BF16)],
        compiler_params=_params(2, VMEM_LIMIT),
        name=name,
    )(oa, o_f, o_b, proj, proj, proj, proj, proj, proj, sg_norm, sg_w, sg_bt, ret_norm, wa, wb, wc)


def _route(logits):
    lane = lax.broadcasted_iota(jnp.int32, logits.shape, 1).astype(F32)
    big = float(1 << 20)
    neg = -jnp.inf
    lgm = jnp.where(lane < N_EXPERT_GROUPS, logits, neg)
    mx = jnp.max(lgm, axis=-1, keepdims=True)
    p_group = 1.0 / jnp.sum(jnp.exp(lgm - mx), axis=-1, keepdims=True)
    gsel = jnp.min(jnp.where(lgm == mx, lane, big), axis=-1, keepdims=True)
    lo = N_EXPERT_GROUPS + gsel * EXPERTS_PER_GROUP
    insel = (lane >= lo) & (lane < lo + EXPERTS_PER_GROUP)
    le = jnp.where(insel, logits, neg)
    v1 = jnp.max(le, axis=-1, keepdims=True)
    i1 = jnp.min(jnp.where(le == v1, lane, big), axis=-1, keepdims=True)
    le2 = jnp.where(lane == i1, neg, le)
    v2 = jnp.max(le2, axis=-1, keepdims=True)
    i2 = jnp.min(jnp.where(le2 == v2, lane, big), axis=-1, keepdims=True)
    e2 = jnp.exp(v2 - v1)
    den = 1.0 + e2
    w1 = p_group * (1.0 / den)
    w2 = p_group * (e2 / den)
    return jnp.where(lane == i1, w1, 0.0) + jnp.where(lane == i2, w2, 0.0), gsel


def _outproj_body(mg_p, x_p, mod_p, mg_s, x_s, mod_s, w_ref, g_ref, wr_ref, br_ref, x1_p, x1_s, h2_ref, rt_ref, *,
                  steps_p):
    def run(mg_ref, x_ref, mod_ref, x1_ref):
        y = jnp.dot(mg_ref[...], w_ref[...], preferred_element_type=F32)
        x1 = x_ref[...] + mod_ref[2:3, :] * y
        x1_ref[...] = x1
        h2 = (_rms(x1) * g_ref[...]) * (1.0 + mod_ref[4:5, :]) + mod_ref[3:4, :]
        h2_ref[...] = h2
        logits = jnp.dot(h2.astype(BF16), wr_ref[...], preferred_element_type=F32) + br_ref[...]
        _, gsel = _route(logits)
        rt_ref[...] = jnp.broadcast_to(gsel, rt_ref.shape)

    i = pl.program_id(0)

    @pl.when(i < steps_p)
    def _():
        run(mg_p, x_p, mod_p, x1_p)

    @pl.when(i >= steps_p)
    def _():
        run(mg_s, x_s, mod_s, x1_s)


def _outproj_call(merged_p, x_p, mod_p, merged_s, x_s, mod_s, w_out, norm2, w_router, b_router, layer, *, t_s, name):
    n_p, d = x_p.shape
    n_s = x_s.shape[0]
    tm = 256
    assert n_p % tm == 0 and t_s % tm == 0
    sp, ss = n_p // tm, n_s // tm
    per_s = t_s // tm
    row_p = pl.BlockSpec((tm, d), lambda i: (jnp.minimum(i, sp - 1), 0))
    row_s = pl.BlockSpec((tm, d), lambda i: (jnp.maximum(i - sp, 0), 0))
    return pl.pallas_call(
        functools.partial(_outproj_body, steps_p=sp),
        grid=(sp + ss,),
        in_specs=[
            row_p, row_p, pl.BlockSpec((None, 8, d), lambda i: (0, 0, 0)),
            row_s, row_s, pl.BlockSpec((None, 8, d), lambda i: (jnp.maximum(i - sp, 0) // per_s, 0, 0)),
            pl.BlockSpec((None, d, d), lambda i: (layer, 0, 0)),
            pl.BlockSpec((None, 1, d), lambda i: (layer, 0, 0)),
            pl.BlockSpec((None, d, ROUTER_LANES), lambda i: (layer, 0, 0)),
            pl.BlockSpec((None, 1, ROUTER_LANES), lambda i: (layer, 0, 0)),
        ],
        out_specs=[row_p, row_s, pl.BlockSpec((tm, d), lambda i: (i, 0)),
                   pl.BlockSpec((tm, ROUTER_LANES), lambda i: (i, 0))],
        out_shape=[
            jax.ShapeDtypeStruct((n_p, d), F32),
            jax.ShapeDtypeStruct((n_s, d), F32),
            jax.ShapeDtypeStruct((n_p + n_s, d), F32),
            jax.ShapeDtypeStruct((n_p + n_s, ROUTER_LANES), F32),
        ],
        compiler_params=_params(1, VMEM_LIMIT),
        name=name,
    )(merged_p, x_p, mod_p, merged_s, x_s, mod_s, w_out, norm2, w_router, b_router)


MOE_FIRST, MOE_LAST, MOE_VALID = 1, 2, 4


def _moe_plan(gsel, tm):
    n = gsel.shape[0]
    nt = n // tm
    n_items = nt + N_EXPERT_GROUPS - 1
    perm = jnp.argsort(gsel, stable=True).astype(jnp.int32)
    gs = gsel[perm]
    gf, gl = gs[0::tm], gs[tm - 1::tm]
    grp = jnp.arange(N_EXPERT_GROUPS, dtype=jnp.int32)[None, :]
    active = ((grp >= gf[:, None]) & (grp <= gl[:, None])).reshape(-1)
    order = jnp.argsort(jnp.logical_not(active), stable=True).astype(jnp.int32)[:n_items]
    valid = active[order]
    last_real = order[jnp.sum(active.astype(jnp.int32)) - 1]
    item = jnp.where(valid, order, last_real)
    wt, wg = item // N_EXPERT_GROUPS, item % N_EXPERT_GROUPS
    flags = (jnp.where(valid, MOE_VALID, 0) + jnp.where(valid & (wg == gf[wt]), MOE_FIRST, 0)
             + jnp.where(valid & (wg == gl[wt]), MOE_LAST, 0)).astype(jnp.int32)
    return perm, wt.astype(jnp.int32), wg.astype(jnp.int32), flags


def _moe_body(perm_ref, wt_ref, wg_ref, fl_ref, h_hbm, wr_ref, br_ref, wgate_ref, wup_ref, wdown_ref, y_hbm,
              h32, hbf, acc, dws, sem, *, tm):
    w = pl.program_id(0)
    e = pl.program_id(1)
    flags = fl_ref[w]
    valid = (flags & MOE_VALID) != 0
    base = wt_ref[w] * tm

    def row_in(r):
        return pltpu.make_async_copy(h_hbm.at[pl.ds(perm_ref[base + r], 1), :], h32.at[pl.ds(r, 1), :], sem.at[0])

    def row_out(r):
        return pltpu.make_async_copy(acc.at[pl.ds(r, 1), :], y_hbm.at[pl.ds(perm_ref[base + r], 1), :], sem.at[1])

    def for_rows(fn):
        def body(r8, c):
            for s in range(8):
                fn(pl.multiple_of(r8 * 8, 8) + s)
            return c
        lax.fori_loop(0, tm // 8, body, 0)

    @pl.when(valid & ((flags & MOE_FIRST) != 0) & (e == 0))
    def _():
        for_rows(lambda r: row_in(r).start())
        for_rows(lambda r: row_in(r).wait())
        hb = h32[...].astype(BF16)
        hbf[...] = hb
        logits = jnp.dot(hb, wr_ref[...], preferred_element_type=F32) + br_ref[...]
        dws[...], _ = _route(logits)
        acc[...] = jnp.zeros(acc.shape, F32)

    @pl.when(valid)
    def _():
        h = hbf[...]
        a = jnp.dot(h, wgate_ref[...].astype(BF16), preferred_element_type=F32)
        up = jnp.dot(h, wup_ref[...].astype(BF16), preferred_element_type=F32)
        lane = lax.broadcasted_iota(jnp.int32, dws.shape, 1)
        ex = N_EXPERT_GROUPS + wg_ref[w] * EXPERTS_PER_GROUP + e
        wgt = jnp.sum(jnp.where(lane == ex, dws[...], 0.0), axis=-1, keepdims=True)
        act = ((_silu(a) * up) * wgt).astype(BF16)
        acc[...] += jnp.dot(act, wdown_ref[...].astype(BF16), preferred_element_type=F32)

    @pl.when(valid & ((flags & MOE_LAST) != 0) & (e == EXPERTS_PER_GROUP - 1))
    def _():
        for_rows(lambda r: row_out(r).start())
        for_rows(lambda r: row_out(r).wait())


def _moe_call(h2_all, grp_all, w_router, b_router, w_gate, w_up, w_down, layer, *, name):
    n, d = h2_all.shape
    f = w_gate.shape[-1]
    tm = 1024 if n % 1024 == 0 and n >= 2048 else 256
    perm, wt, wg, flags = _moe_plan(grp_all[:, 0].astype(jnp.int32), tm)
    n_items = wt.shape[0]

    def expert(w, e, perm_ref, wt_ref, wg_ref, fl_ref):
        e_eff = jnp.where((fl_ref[w] & MOE_VALID) != 0, e, EXPERTS_PER_GROUP - 1)
        return (layer, wg_ref[w] * EXPERTS_PER_GROUP + e_eff, 0, 0)

    return pl.pallas_call(
        functools.partial(_moe_body, tm=tm),
        grid_spec=pltpu.PrefetchScalarGridSpec(
            num_scalar_prefetch=4,
            grid=(n_items, EXPERTS_PER_GROUP),
            in_specs=[
                pl.BlockSpec(memory_space=pl.ANY),
                pl.BlockSpec((None, d, ROUTER_LANES), lambda w, e, *_: (layer, 0, 0)),
                pl.BlockSpec((None, 1, ROUTER_LANES), lambda w, e, *_: (layer, 0, 0)),
                pl.BlockSpec((None, None, d, f), expert),
                pl.BlockSpec((None, None, d, f), expert),
                pl.BlockSpec((None, None, f, d), expert),
            ],
            out_specs=pl.BlockSpec(memory_space=pl.ANY),
            scratch_shapes=[
                pltpu.VMEM((tm, d), F32),
                pltpu.VMEM((tm, d), BF16),
                pltpu.VMEM((tm, d), F32),
                pltpu.VMEM((tm, ROUTER_LANES), F32),
                pltpu.SemaphoreType.DMA((2,)),
            ],
        ),
        out_shape=jax.ShapeDtypeStruct((n, d), F32),
        compiler_params=_params(2, VMEM_LIMIT),
        name=name,
    )(perm, wt, wg, flags, h2_all, w_router, b_router, w_gate, w_up, w_down)


def _rope_tables(n_tok):
    rows = n_tok // GRID_W
    row = jnp.repeat(jnp.arange(rows, dtype=F32), GRID_W)
    col = jnp.tile(jnp.arange(GRID_W, dtype=F32), rows)
    n_freq = ROPE_DIM // 4
    inv = ROPE_BASE ** (-jnp.arange(n_freq, dtype=F32) / n_freq)
    ang = jnp.concatenate([row[:, None] * inv, col[:, None] * inv], axis=-1)
    cos, sin = jnp.cos(ang), jnp.sin(ang)
    return jnp.concatenate([cos, cos, cos, cos], axis=-1), jnp.concatenate([-sin, sin, -sin, sin], axis=-1)


def kernel(x_prompt, x_sample, cache_k, cache_v, state_ret, c, c_ctx, w_mod, b_mod, norm1, w_in, lambda_p, diff_norm, sg_norm, sg_w, sg_b, ret_decay, ret_norm, w_up_a, w_up_b, w_up_c, w_out, norm2, w_rg, b_rg, w_re, b_re, w_e_gate, w_e_up, w_e_down, final_norm):
    depth = w_in.shape[0]
    bp, tp, d = x_prompt.shape
    bs, ts, _ = x_sample.shape
    past = cache_k.shape[2]

    cond8 = jnp.zeros((8, d), F32).at[0].set(c_ctx).at[1:1 + bs].set(c)
    mods = _mod_call(cond8, w_mod, b_mod).reshape(depth, 8, N_MOD, d)
    mods = jnp.pad(mods, ((0, 0), (0, 0), (0, 8 - N_MOD), (0, 0)))

    wa_b, wb_b, wc_b, wo_b = (w.astype(BF16) for w in (w_up_a, w_up_b, w_up_c, w_out))
    w_router = jnp.concatenate(
        [w_rg, w_re, jnp.zeros((depth, d, ROUTER_LANES - N_EXPERT_GROUPS - N_EXPERTS), F32)], axis=-1).astype(BF16)
    b_router = jnp.concatenate(
        [b_rg, b_re, jnp.zeros((depth, ROUTER_LANES - N_EXPERT_GROUPS - N_EXPERTS), F32)], axis=-1)[:, None, :]
    log_g = jax.nn.log_sigmoid(ret_decay.astype(F32))
    sg_bt = jnp.swapaxes(sg_b, 1, 2)
    rope_tabs = _rope_tables(ts)
    cache_k2 = cache_k.reshape(bs, depth, past, W_QA)
    cache_v2 = cache_v.reshape(bs, depth, past, W_VA)

    groups = [
        dict(tag="p", x=x_prompt.reshape(bp * tp, d), nb=bp, t=tp, tt=bp * tp, rope=None, ctx=False, hb_a=8, hb_r=8),
        dict(tag="s", x=x_sample.reshape(bs * ts, d), nb=bs, t=ts, tt=ts, rope=rope_tabs, ctx=True, hb_a=1, hb_r=2),
    ]
    ks, vs, ss = [], [], []
    row0 = 0
    for g in groups:
        g["res"] = None
        g["row0"] = row0
        row0 += g["nb"] * g["t"]
    for l in range(depth):
        lam_init = 0.8 - 0.6 * math.exp(-0.3 * l)
        for g in groups:
            tag, nb, t, tt = g["tag"], g["nb"], g["t"], g["tt"]
            mod = mods[l, 0:1] if not g["ctx"] else mods[l, 1:1 + nb]
            x, h = _norm_call(g["x"], norm1[l], mod, g["res"], t_batch=tt, name=f"norm1_{tag}{l}")
            proj = _inproj_call(h, w_in, l, g["rope"], t_batch=tt, name=f"inproj_{tag}{l}")
            cache = (cache_k2, cache_v2) if g["ctx"] else None
            oa = _attn_call(proj, lambda_p, diff_norm[:, None, :], cache, l, n_batch=nb, t_batch=t,
                            lam_init=lam_init, hb=g["hb_a"], name=f"attn_{tag}{l}")
            o_f, o_b, s_fin = _ret_call(proj, log_g[l], state_ret if g["ctx"] else None, l, n_batch=nb, t_batch=t,
                                        hb=g["hb_r"], name=f"ret_{tag}{l}")
            merged = _mix_call(oa, o_f, o_b, proj, sg_norm[:, None, :], sg_w, sg_bt, ret_norm[:, None, :],
                               wa_b, wb_b, wc_b, l, t_batch=tt, name=f"mix_{tag}{l}")
            g["x"], g["mod"], g["merged"] = x, mod, merged
            if not g["ctx"]:
                ks.append(proj[:, OFF_KA:OFF_KA + W_QA].reshape(nb, t, N_HEADS_A, 2, D_HEAD_A))
                vs.append(proj[:, OFF_VA:OFF_VA + W_VA].reshape(nb, t, N_HEADS_A, D_V_A))
                ss.append(s_fin)
        gp, gs_ = groups
        gp["x"], gs_["x"], h2_all, grp_all = _outproj_call(
            gp["merged"], gp["x"], gp["mod"], gs_["merged"], gs_["x"], gs_["mod"], wo_b, norm2[:, None, :],
            w_router, b_router, l, t_s=gs_["t"], name=f"outproj{l}")
        y_all = _moe_call(h2_all, grp_all, w_router, b_router, w_e_gate, w_e_up, w_e_down, l, name=f"moe{l}")
        for g in groups:
            g["res"] = (y_all, g["row0"], g["mod"])
    outs = []
    for g in groups:
        yn = _norm_call(g["x"], final_norm, None, g["res"], t_batch=g["tt"], name=f"final_{g['tag']}")
        outs.append(yn.reshape(g["nb"], g["t"], d))
    return (outs[0], outs[1], jnp.stack(ks, axis=1), jnp.stack(vs, axis=1), jnp.stack(ss, axis=1))
```

```python
import functools
import math

import jax
import jax.numpy as jnp
from jax import lax
from jax.experimental import pallas as pl
from jax.experimental.pallas import tpu as pltpu

F32 = jnp.float32
BF16 = jnp.bfloat16

GRID_W = 64
CHUNK = 128
N_HEADS_A = 8
D_HEAD_A = 64
D_V_A = 128
N_GROUPS_B = 8
D_GROUP_B = 128
N_HEADS_R = 8
D_K_R = 64
D_V_R = 128
ROPE_DIM = 64
ROPE_BASE = 10000.0
N_EXPERT_GROUPS = 4
EXPERTS_PER_GROUP = 4
N_EXPERTS = 16
D_EXPERT = 256
N_MOD = 6
EPS = 1e-6

W_QA = N_HEADS_A * 2 * D_HEAD_A
W_VA = N_HEADS_A * D_V_A
W_B = N_GROUPS_B * D_GROUP_B
W_QR = N_HEADS_R * D_K_R
W_VR = N_HEADS_R * D_V_R

OFF_QA = 0
OFF_KA = OFF_QA + W_QA
OFF_VA = OFF_KA + W_QA
OFF_U = OFF_VA + W_VA
OFF_V = OFF_U + W_B
OFF_QR = OFF_V + W_B
OFF_KR = OFF_QR + W_QR
OFF_VR = OFF_KR + W_QR
OFF_GR = OFF_VR + W_VR
OFF_GA = OFF_GR + W_VR

LANES = 128
RET_TILE = 256
ROUTER_LANES = 128
VMEM_LIMIT = 56 * 1024 * 1024


def _params(n_axes, vmem=None):
    return pltpu.CompilerParams(dimension_semantics=("arbitrary",) * n_axes, vmem_limit_bytes=vmem)


def _sigmoid(x):
    return 1.0 / (1.0 + jnp.exp(-x))


def _silu(x):
    return x * _sigmoid(x)


def _gelu_tanh(x):
    return 0.5 * x * (1.0 + jnp.tanh(math.sqrt(2.0 / math.pi) * (x + 0.044715 * (x * x * x))))


def _rms(x):
    return x * lax.rsqrt(jnp.mean(x * x, axis=-1, keepdims=True) + EPS)


def _mod_body(c_ref, w_ref, b_ref, o_ref):
    a = _silu(c_ref[...]).astype(BF16)
    o_ref[...] = jnp.dot(a, w_ref[...].astype(BF16), preferred_element_type=F32) + b_ref[...]


def _mod_call(cond8, w_mod, b_mod):
    depth, d, n6 = w_mod.shape
    tn = 1024
    return pl.pallas_call(
        _mod_body,
        grid=(depth, n6 // tn),
        in_specs=[
            pl.BlockSpec((8, d), lambda l, j: (0, 0)),
            pl.BlockSpec((None, d, tn), lambda l, j: (l, 0, j)),
            pl.BlockSpec((None, 1, tn), lambda l, j: (l, 0, j)),
        ],
        out_specs=pl.BlockSpec((None, 8, tn), lambda l, j: (l, 0, j)),
        out_shape=jax.ShapeDtypeStruct((depth, 8, n6), F32),
        compiler_params=_params(2, VMEM_LIMIT),
        name="mod_vectors",
    )(cond8, w_mod, b_mod.reshape(depth, 1, n6))


def _norm_body(*refs, has_res, adaln):
    it = iter(refs)
    x_ref = next(it)
    if has_res:
        y_ref = next(it)
        modp_ref = next(it)
    g_ref = next(it)
    if adaln:
        mod_ref = next(it)
    if has_res and adaln:
        xo_ref = next(it)
    h_ref = next(it)

    x = x_ref[...]
    if has_res:
        x = x + modp_ref[5:6, :] * y_ref[...]
        if adaln:
            xo_ref[...] = x
    y = _rms(x) * g_ref[...]
    if adaln:
        y = y * (1.0 + mod_ref[1:2, :]) + mod_ref[0:1, :]
    h_ref[...] = y.astype(h_ref.dtype)


def _norm_call(x, gain, mod=None, res=None, *, t_batch, name):
    n, d = x.shape
    tm = min(512, t_batch)
    per = t_batch // tm
    adaln = mod is not None
    has_res = res is not None
    row = pl.BlockSpec((tm, d), lambda i: (i, 0))
    modspec = pl.BlockSpec((None, 8, d), lambda i: (i // per, 0, 0))
    args, specs = [x], [row]
    if has_res:
        y_all, row0, mod_prev = res
        assert row0 % tm == 0
        args += [y_all, mod_prev]
        specs += [pl.BlockSpec((tm, d), lambda i: (i + row0 // tm, 0)), modspec]
    args.append(gain.reshape(1, d))
    specs.append(pl.BlockSpec((1, d), lambda i: (0, 0)))
    if adaln:
        args.append(mod)
        specs.append(modspec)
    out_shape, out_specs = [], []
    if has_res and adaln:
        out_shape.append(jax.ShapeDtypeStruct((n, d), F32))
        out_specs.append(row)
    out_shape.append(jax.ShapeDtypeStruct((n, d), BF16 if adaln else F32))
    out_specs.append(row)
    outs = pl.pallas_call(
        functools.partial(_norm_body, has_res=has_res, adaln=adaln),
        grid=(n // tm,),
        in_specs=specs,
        out_specs=out_specs,
        out_shape=out_shape,
        compiler_params=_params(1, VMEM_LIMIT),
        name=name,
    )(*args)
    if has_res and adaln:
        return outs[0], outs[1]
    return (x, outs[0]) if adaln else outs[0]


def _rope_swap(blk):
    lane = lax.broadcasted_iota(jnp.int32, blk.shape, 1)
    return jnp.where((lane & 32) == 0, pltpu.roll(blk, LANES - 32, 1), pltpu.roll(blk, 32, 1))


def _inproj_body(*refs, rope, emit_kv, tn):
    if rope:
        h_ref, w_ref, cs_ref, sn_ref, o_ref, wbf_ref = refs
    elif emit_kv:
        h_ref, w_ref, o_ref, ka_ref, va_ref, wbf_ref = refs
    else:
        h_ref, w_ref, o_ref, wbf_ref = refs
    n = pl.program_id(0)
    m = pl.program_id(1)

    @pl.when(m == 0)
    def _():
        wbf_ref[...] = w_ref[...].astype(BF16)

    o_ref[...] = jnp.dot(h_ref[...], wbf_ref[...], preferred_element_type=F32)

    def fix(j, scale):
        blk = o_ref[:, j * LANES:(j + 1) * LANES]
        if scale != 1.0:
            blk = blk * scale
        if rope:
            blk = blk * cs_ref[...] + _rope_swap(blk) * sn_ref[...]
        o_ref[:, j * LANES:(j + 1) * LANES] = blk

    groups = tn // LANES
    if rope:
        @pl.when(n < (OFF_VA // tn))
        def _():
            for j in range(groups):
                fix(j, 1.0)

    @pl.when(n == (OFF_QR // tn))
    def _():
        for j in range(groups):
            is_k = j * LANES >= W_QR
            if rope or is_k:
                fix(j, D_K_R ** -0.5 if is_k else 1.0)

    if emit_kv:
        @pl.when(n == OFF_KA // tn)
        def _():
            ka_ref[...] = o_ref[...]

        @pl.when(n == OFF_VA // tn)
        def _():
            va_ref[...] = o_ref[...]


def _inproj_call(h, w_in, layer, rope_tabs, *, t_batch, emit_kv, name):
    n_tok, d = h.shape
    d_in = w_in.shape[-1]
    tn = 1024
    assert OFF_QR % tn == 0 and OFF_QR + 2 * W_QR == OFF_QR + tn and d_in % tn == 0
    assert W_QA == tn and W_VA == tn
    tm = min(512 if emit_kv else 1024, t_batch)
    per = t_batch // tm
    m_tiles = n_tok // tm
    rope = rope_tabs is not None
    assert not (rope and emit_kv)
    args = [h, w_in]
    specs = [
        pl.BlockSpec((tm, d), lambda n, m: (m, 0)),
        pl.BlockSpec((None, d, tn), lambda n, m: (layer, 0, n)),
    ]
    if rope:
        tab = pl.BlockSpec((tm, LANES), lambda n, m: (m % per, 0))
        args += list(rope_tabs)
        specs += [tab, tab]
    out_specs = [pl.BlockSpec((tm, tn), lambda n, m: (m, n))]
    out_shape = [jax.ShapeDtypeStruct((n_tok, d_in), F32)]
    if emit_kv:
        def parked(col):
            return lambda n, m: (jnp.where(n < col, 0, jnp.where(n == col, m, m_tiles - 1)), 0)

        out_specs += [pl.BlockSpec((tm, tn), parked(OFF_KA // tn)), pl.BlockSpec((tm, tn), parked(OFF_VA // tn))]
        out_shape += [jax.ShapeDtypeStruct((n_tok, tn), F32)] * 2
    outs = pl.pallas_call(
        functools.partial(_inproj_body, rope=rope, emit_kv=emit_kv, tn=tn),
        grid=(d_in // tn, m_tiles),
        in_specs=specs,
        out_specs=out_specs,
        out_shape=out_shape,
        scratch_shapes=[pltpu.VMEM((d, tn), BF16)],
        compiler_params=_params(2, VMEM_LIMIT),
        name=name,
    )(*args)
    return outs if emit_kv else outs[0]


def _attn_body(*refs, hb, t_new, past, lam_init):
    if past:
        lp_ref, q_ref, k_ref, v_ref, ck_ref, cv_ref, g_ref, o_ref, kbf, vbf = refs
    else:
        lp_ref, q_ref, k_ref, v_ref, g_ref, o_ref, kbf, vbf = refs
    i = pl.program_id(2)

    @pl.when(i == 0)
    def _():
        if past:
            kbf[0:past, :] = ck_ref[...].astype(BF16)
            vbf[0:past, :] = cv_ref[...].astype(BF16)
        kbf[past:past + t_new, :] = k_ref[...].astype(BF16)
        vbf[past:past + t_new, :] = v_ref[...].astype(BF16)

    lp = lp_ref[...]
    lam = (jnp.exp(jnp.sum(lp[0:1, :] * lp[1:2, :], keepdims=True))
           - jnp.exp(jnp.sum(lp[2:3, :] * lp[3:4, :], keepdims=True)) + lam_init)
    nt = (((1,), (1,)), ((), ()))
    for j in range(hb):
        sl = slice(j * LANES, (j + 1) * LANES)
        q = q_ref[:, sl] * (D_HEAD_A ** -0.5)
        lane = lax.broadcasted_iota(jnp.int32, q.shape, 1)
        q0 = jnp.where(lane < D_HEAD_A, q, 0.0).astype(BF16)
        q1 = jnp.where(lane >= D_HEAD_A, q, 0.0).astype(BF16)
        k = kbf[:, sl]
        s0 = lax.dot_general(q0, k, nt, preferred_element_type=F32)
        s1 = lax.dot_general(q1, k, nt, preferred_element_type=F32)
        p0 = jnp.exp(s0 - jnp.max(s0, axis=-1, keepdims=True))
        p1 = jnp.exp(s1 - jnp.max(s1, axis=-1, keepdims=True))
        r0 = 1.0 / jnp.sum(p0, axis=-1, keepdims=True)
        r1 = lam / jnp.sum(p1, axis=-1, keepdims=True)
        a = (p0 * r0 - p1 * r1).astype(BF16)
        o = jnp.dot(a, vbf[:, sl], preferred_element_type=F32)
        o_ref[:, sl] = ((_rms(o) * g_ref[...]) * (1.0 - lam_init)).astype(o_ref.dtype)


def _attn_pipe_body(*refs, t_new, past, lam_init):
    if past:
        lp_ref, q_ref, k_ref, v_ref, ck_ref, cv_ref, g_ref, o_ref, kbf, vbf, s_a, m_a, s_b, m_b = refs
    else:
        lp_ref, q_ref, k_ref, v_ref, g_ref, o_ref, kbf, vbf, s_a, m_a, s_b, m_b = refs
    i = pl.program_id(2)

    @pl.when(i == 0)
    def _():
        if past:
            kbf[0:past, :] = ck_ref[...].astype(BF16)
            vbf[0:past, :] = cv_ref[...].astype(BF16)
        kbf[past:past + t_new, :] = k_ref[...].astype(BF16)
        vbf[past:past + t_new, :] = v_ref[...].astype(BF16)
        s_b[...] = jnp.zeros(s_b.shape, F32)
        m_b[...] = jnp.zeros(m_b.shape, F32)

    lp = lp_ref[...]
    lam = (jnp.exp(jnp.sum(lp[0:1, :] * lp[1:2, :], keepdims=True))
           - jnp.exp(jnp.sum(lp[2:3, :] * lp[3:4, :], keepdims=True)) + lam_init)
    nt = (((1,), (1,)), ((), ()))

    def step(s_w, m_w, s_r, m_r):
        q = q_ref[...] * (D_HEAD_A ** -0.5)
        lane = lax.broadcasted_iota(jnp.int32, q.shape, 1)
        k = kbf[...]
        for mp in range(2):
            qm = jnp.where((lane >= mp * D_HEAD_A) & (lane < (mp + 1) * D_HEAD_A), q, 0.0).astype(BF16)
            s = lax.dot_general(qm, k, nt, preferred_element_type=F32)
            s_w[mp] = s
            m_w[mp] = jnp.max(s, axis=-1, keepdims=True)
        p0 = jnp.exp(s_r[0] - m_r[0])
        p1 = jnp.exp(s_r[1] - m_r[1])
        r0 = 1.0 / jnp.sum(p0, axis=-1, keepdims=True)
        r1 = lam / jnp.sum(p1, axis=-1, keepdims=True)
        a = (p0 * r0 - p1 * r1).astype(BF16)
        o = jnp.dot(a, vbf[...], preferred_element_type=F32)
        o_ref[...] = ((_rms(o) * g_ref[...]) * (1.0 - lam_init)).astype(o_ref.dtype)

    @pl.when(i % 2 == 0)
    def _():
        step(s_a, m_a, s_b, m_b)

    @pl.when(i % 2 == 1)
    def _():
        step(s_b, m_b, s_a, m_a)


def _attn_call(proj, lambda_p, diff_norm, cache, layer, *, n_batch, t_batch, lam_init, hb, name):
    n_tok = proj.shape[0]
    tq = min(256, t_batch)
    qsteps = t_batch // tq
    w = hb * LANES
    past = 0 if cache is None else cache[0].shape[2]
    pipe = hb == 1 and qsteps > 1
    if pipe:
        def qrow(b, i):
            return b * qsteps + jnp.minimum(i, qsteps - 1)

        def orow(b, i):
            return b * qsteps + jnp.maximum(i - 1, 0)
    else:
        def qrow(b, i):
            return b * qsteps + i
        orow = qrow
    args = [lambda_p, proj, proj, proj]
    specs = [
        pl.BlockSpec((None, 4, D_HEAD_A), lambda b, h, i: (layer, 0, 0)),
        pl.BlockSpec((tq, w), lambda b, h, i: (qrow(b, i), OFF_QA // w + h)),
        pl.BlockSpec((t_batch, w), lambda b, h, i: (b, OFF_KA // w + h)),
        pl.BlockSpec((t_batch, w), lambda b, h, i: (b, OFF_VA // w + h)),
    ]
    if past:
        cspec = pl.BlockSpec((None, None, past, w), lambda b, h, i: (b, layer, 0, h))
        args += [cache[0], cache[1]]
        specs += [cspec, cspec]
    args.append(diff_norm)
    specs.append(pl.BlockSpec((None, 1, D_V_A), lambda b, h, i: (layer, 0, 0)))
    t_keys = past + t_batch
    scratch = [pltpu.VMEM((t_keys, w), BF16), pltpu.VMEM((t_keys, w), BF16)]
    if pipe:
        body = functools.partial(_attn_pipe_body, t_new=t_batch, past=past, lam_init=lam_init)
        scratch += [pltpu.VMEM((2, tq, t_keys), F32), pltpu.VMEM((2, tq, 1), F32)] * 2
    else:
        body = functools.partial(_attn_body, hb=hb, t_new=t_batch, past=past, lam_init=lam_init)
    return pl.pallas_call(
        body,
        grid=(n_batch, N_HEADS_A // hb, qsteps + (1 if pipe else 0)),
        in_specs=specs,
        out_specs=pl.BlockSpec((tq, w), lambda b, h, i: (orow(b, i), h)),
        out_shape=jax.ShapeDtypeStruct((n_tok, W_VA), BF16),
        scratch_shapes=scratch,
        compiler_params=_params(3, VMEM_LIMIT),
        name=name,
    )(*args)


def _ret_body(*refs, hb, nc, c_, has_state):
    if has_state:
        lg_ref, q_ref, k_ref, v_ref, s0_ref, of_ref, ob_ref, sf_ref, st_ref, dec_ref, vec_ref, cd_ref = refs
    else:
        lg_ref, q_ref, k_ref, v_ref, of_ref, ob_ref, sf_ref, st_ref, dec_ref, vec_ref, cd_ref = refs
    hblk = pl.program_id(1)
    half = D_K_R

    rel = (lax.broadcasted_iota(jnp.int32, (c_, c_), 0) - lax.broadcasted_iota(jnp.int32, (c_, c_), 1)).astype(F32)
    pos = lax.broadcasted_iota(jnp.int32, (c_, LANES), 0).astype(F32)
    for j in range(hb):
        for d in range(2):
            lg = lg_ref[d, hblk * hb + j]
            if d == 0:
                dec_ref[d, j] = jnp.where(rel >= 0, jnp.exp(jnp.maximum(rel, 0.0) * lg), 0.0)
                vec_ref[d, j, 0] = jnp.exp((pos + 1.0) * lg)
                vec_ref[d, j, 1] = jnp.exp((c_ - 1.0 - pos) * lg)
            else:
                dec_ref[d, j] = jnp.where(rel <= 0, jnp.exp(jnp.maximum(-rel, 0.0) * lg), 0.0)
                vec_ref[d, j, 0] = jnp.exp((c_ - pos) * lg)
                vec_ref[d, j, 1] = jnp.exp(pos * lg)
            cd_ref[d, j] = jnp.exp(jnp.full((LANES, D_V_R), float(c_), F32) * lg)
            st_ref[d, j] = jnp.zeros((LANES, D_V_R), F32)
            if has_state:
                lo = (j % 2) * half
                st_ref[d, j, lo:lo + half, :] = s0_ref[d, j]

    nt = (((1,), (1,)), ((), ()))

    def step(i, carry):
        for d in range(2):
            c = i if d == 0 else nc - 1 - i
            r0 = pl.multiple_of(c * c_, c_)
            for j in range(hb):
                jp = j // 2
                q = q_ref[pl.ds(r0, c_), jp * LANES:(jp + 1) * LANES]
                k = k_ref[pl.ds(r0, c_), jp * LANES:(jp + 1) * LANES]
                v = v_ref[pl.ds(r0, c_), j * D_V_R:(j + 1) * D_V_R].astype(BF16)
                lane = lax.broadcasted_iota(jnp.int32, q.shape, 1)
                mine = (lane >= (j % 2) * half) & (lane < (j % 2 + 1) * half)
                qm = jnp.where(mine, q, 0.0).astype(BF16)
                s = lax.dot_general(qm, k.astype(BF16), nt, preferred_element_type=F32) * dec_ref[d, j]
                slab = st_ref[d, j]
                o = (jnp.dot(s.astype(BF16), v, preferred_element_type=F32)
                     + jnp.dot(qm, slab.astype(BF16), preferred_element_type=F32) * vec_ref[d, j, 0])
                kzt = (k * vec_ref[d, j, 1]).T.astype(BF16)
                st_ref[d, j] = slab * cd_ref[d, j] + jnp.dot(kzt, v, preferred_element_type=F32)
                if d == 0:
                    of_ref[pl.ds(r0, c_), j * D_V_R:(j + 1) * D_V_R] = o
                else:
                    ob_ref[pl.ds(r0, c_), j * D_V_R:(j + 1) * D_V_R] = o
        return carry

    lax.fori_loop(0, nc, step, 0, unroll=4 if nc % 4 == 0 else 1)
    for j in range(hb):
        lo = (j % 2) * half
        for d in range(2):
            sf_ref[d, j] = st_ref[d, j, lo:lo + half, :]


def _ret_call(proj, log_g, state, layer, *, n_batch, t_batch, hb, name):
    n_tok = proj.shape[0]
    c_ = min(RET_TILE, t_batch)
    nc = t_batch // c_
    wq, wv = hb * D_K_R, hb * D_V_R
    args = [log_g, proj, proj, proj]
    specs = [
        pl.BlockSpec(memory_space=pltpu.SMEM),
        pl.BlockSpec((t_batch, wq), lambda b, h: (b, OFF_QR // wq + h)),
        pl.BlockSpec((t_batch, wq), lambda b, h: (b, OFF_KR // wq + h)),
        pl.BlockSpec((t_batch, wv), lambda b, h: (b, OFF_VR // wv + h)),
    ]
    has_state = state is not None
    if has_state:
        args.append(state)
        specs.append(pl.BlockSpec((None, None, 2, hb, D_K_R, D_V_R), lambda b, h: (b, layer, 0, h, 0, 0)))
    ospec = pl.BlockSpec((t_batch, wv), lambda b, h: (b, h))
    return pl.pallas_call(
        functools.partial(_ret_body, hb=hb, nc=nc, c_=c_, has_state=has_state),
        grid=(n_batch, N_HEADS_R // hb),
        in_specs=specs,
        out_specs=[ospec, ospec, pl.BlockSpec((None, 2, hb, D_K_R, D_V_R), lambda b, h: (b, 0, h, 0, 0))],
        out_shape=[
            jax.ShapeDtypeStruct((n_tok, W_VR), F32),
            jax.ShapeDtypeStruct((n_tok, W_VR), F32),
            jax.ShapeDtypeStruct((n_batch, 2, N_HEADS_R, D_K_R, D_V_R), F32),
        ],
        scratch_shapes=[
            pltpu.VMEM((2, hb, LANES, D_V_R), F32),
            pltpu.VMEM((2, hb, c_, c_), F32),
            pltpu.VMEM((2, hb, 2, c_, LANES), F32),
            pltpu.VMEM((2, hb, LANES, D_V_R), F32),
        ],
        compiler_params=_params(2, VMEM_LIMIT),
        name=name,
    )(*args)


def _mix_body(oa_ref, of_ref, ob_ref, u_ref, v_ref, gr_ref, ga_ref, gb_ref, gc_ref, sgn_ref, sgw_ref, sgb_ref,
              rn_ref, wa_ref, wb_ref, wc_ref, o_ref, sb_s, oc_s, *, tm):
    for j in range(N_HEADS_R):
        sl = slice(j * D_V_R, (j + 1) * D_V_R)
        y = _rms(of_ref[:, sl] + ob_ref[:, sl]) * rn_ref[:, sl]
        oc_s[:, sl] = (_silu(gr_ref[:, sl]) * y).astype(BF16)
    gv = _gelu_tanh(v_ref[...])
    vn = (_rms(gv) * sgn_ref[...]).astype(BF16)
    for g in range(N_GROUPS_B):
        sl = slice(g * D_GROUP_B, (g + 1) * D_GROUP_B)
        wg = sgw_ref[g].astype(BF16)
        bias = sgb_ref[:, g:g + 1]
        for c in range(tm // CHUNK):
            rows = slice(c * CHUNK, (c + 1) * CHUNK)
            mixed = jnp.dot(wg, vn[rows, sl], preferred_element_type=F32) + bias
            sb_s[rows, sl] = (_gelu_tanh(u_ref[rows, sl]) * mixed).astype(BF16)

    ya = jnp.dot(oa_ref[...], wa_ref[...], preferred_element_type=F32)
    yb = jnp.dot(sb_s[...], wb_ref[...], preferred_element_type=F32)
    yc = jnp.dot(oc_s[...], wc_ref[...], preferred_element_type=F32)
    merged = _sigmoid(ga_ref[...]) * ya + _sigmoid(gb_ref[...]) * yb + _sigmoid(gc_ref[...]) * yc
    o_ref[...] = merged.astype(o_ref.dtype)


def _mix_call(oa, o_f, o_b, proj, sg_norm, sg_w, sg_bt, ret_norm, wa, wb, wc, layer, *, t_batch, name):
    n_tok = oa.shape[0]
    d = wa.shape[-1]
    tm = min(256, t_batch)
    wide = pl.BlockSpec((tm, W_B), lambda m: (m, 0))

    def pcol(off, width):
        return pl.BlockSpec((tm, width), lambda m: (m, off // width))

    def lvec(width):
        return pl.BlockSpec((None, 1, width), lambda m: (layer, 0, 0))

    wspec = pl.BlockSpec((None, W_B, d), lambda m: (layer, 0, 0), pipeline_mode=pl.Buffered(1))
    return pl.pallas_call(
        functools.partial(_mix_body, tm=tm),
        grid=(n_tok // tm,),
        in_specs=[
            wide, wide, wide, pcol(OFF_U, W_B), pcol(OFF_V, W_B), pcol(OFF_GR, W_VR),
            pcol(OFF_GA, d), pcol(OFF_GA + d, d), pcol(OFF_GA + 2 * d, d),
            lvec(W_B),
            pl.BlockSpec((None, N_GROUPS_B, CHUNK, CHUNK), lambda m: (layer, 0, 0, 0)),
            pl.BlockSpec((None, CHUNK, N_GROUPS_B), lambda m: (layer, 0, 0)),
            lvec(W_VR), wspec, wspec, wspec,
        ],
        out_specs=pl.BlockSpec((tm, d), lambda m: (m, 0)),
        out_shape=jax.ShapeDtypeStruct((n_tok, d), BF16),
        scratch_shapes=[pltpu.VMEM((tm, W_B), BF16), pltpu.VMEM((tm, W_VR), BF16)],
        compiler_params=_params(1, VMEM_LIMIT),
        name=name,
    )(oa, o_f, o_b, proj, proj, proj, proj, proj, proj, sg_norm, sg_w, sg_bt, ret_norm, wa, wb, wc)


def _route(logits):
    lane = lax.broadcasted_iota(jnp.int32, logits.shape, 1).astype(F32)
    big = float(1 << 20)
    neg = -jnp.inf
    lgm = jnp.where(lane < N_EXPERT_GROUPS, logits, neg)
    mx = jnp.max(lgm, axis=-1, keepdims=True)
    p_group = 1.0 / jnp.sum(jnp.exp(lgm - mx), axis=-1, keepdims=True)
    gsel = jnp.min(jnp.where(lgm == mx, lane, big), axis=-1, keepdims=True)
    lo = N_EXPERT_GROUPS + gsel * EXPERTS_PER_GROUP
    insel = (lane >= lo) & (lane < lo + EXPERTS_PER_GROUP)
    le = jnp.where(insel, logits, neg)
    v1 = jnp.max(le, axis=-1, keepdims=True)
    i1 = jnp.min(jnp.where(le == v1, lane, big), axis=-1, keepdims=True)
    le2 = jnp.where(lane == i1, neg, le)
    v2 = jnp.max(le2, axis=-1, keepdims=True)
    i2 = jnp.min(jnp.where(le2 == v2, lane, big), axis=-1, keepdims=True)
    e2 = jnp.exp(v2 - v1)
    den = 1.0 + e2
    w1 = p_group * (1.0 / den)
    w2 = p_group * (e2 / den)
    return jnp.where(lane == i1, w1, 0.0) + jnp.where(lane == i2, w2, 0.0), gsel


def _outproj_body(mg_p, x_p, mod_p, mg_s, x_s, mod_s, w_ref, g_ref, wr_ref, br_ref, x1_p, x1_s, h2_ref, rt_ref, *,
                  steps_p):
    def run(mg_ref, x_ref, mod_ref, x1_ref):
        y = jnp.dot(mg_ref[...], w_ref[...], preferred_element_type=F32)
        x1 = x_ref[...] + mod_ref[2:3, :] * y
        x1_ref[...] = x1
        h2 = (_rms(x1) * g_ref[...]) * (1.0 + mod_ref[4:5, :]) + mod_ref[3:4, :]
        h2_ref[...] = h2
        logits = jnp.dot(h2.astype(BF16), wr_ref[...], preferred_element_type=F32) + br_ref[...]
        _, gsel = _route(logits)
        rt_ref[...] = jnp.broadcast_to(gsel, rt_ref.shape)

    i = pl.program_id(0)

    @pl.when(i < steps_p)
    def _():
        run(mg_p, x_p, mod_p, x1_p)

    @pl.when(i >= steps_p)
    def _():
        run(mg_s, x_s, mod_s, x1_s)


def _outproj_call(merged_p, x_p, mod_p, merged_s, x_s, mod_s, w_out, norm2, w_router, b_router, layer, *, t_s, name):
    n_p, d = x_p.shape
    n_s = x_s.shape[0]
    tm = 256
    assert n_p % tm == 0 and t_s % tm == 0
    sp, ss = n_p // tm, n_s // tm
    per_s = t_s // tm
    row_p = pl.BlockSpec((tm, d), lambda i: (jnp.minimum(i, sp - 1), 0))
    row_s = pl.BlockSpec((tm, d), lambda i: (jnp.maximum(i - sp, 0), 0))
    return pl.pallas_call(
        functools.partial(_outproj_body, steps_p=sp),
        grid=(sp + ss,),
        in_specs=[
            row_p, row_p, pl.BlockSpec((None, 8, d), lambda i: (0, 0, 0)),
            row_s, row_s, pl.BlockSpec((None, 8, d), lambda i: (jnp.maximum(i - sp, 0) // per_s, 0, 0)),
            pl.BlockSpec((None, d, d), lambda i: (layer, 0, 0)),
            pl.BlockSpec((None, 1, d), lambda i: (layer, 0, 0)),
            pl.BlockSpec((None, d, ROUTER_LANES), lambda i: (layer, 0, 0)),
            pl.BlockSpec((None, 1, ROUTER_LANES), lambda i: (layer, 0, 0)),
        ],
        out_specs=[row_p, row_s, pl.BlockSpec((tm, d), lambda i: (i, 0)),
                   pl.BlockSpec((tm, ROUTER_LANES), lambda i: (i, 0))],
        out_shape=[
            jax.ShapeDtypeStruct((n_p, d), F32),
            jax.ShapeDtypeStruct((n_s, d), F32),
            jax.ShapeDtypeStruct((n_p + n_s, d), F32),
            jax.ShapeDtypeStruct((n_p + n_s, ROUTER_LANES), F32),
        ],
        compiler_params=_params(1, VMEM_LIMIT),
        name=name,
    )(merged_p, x_p, mod_p, merged_s, x_s, mod_s, w_out, norm2, w_router, b_router)


MOE_FIRST, MOE_LAST, MOE_VALID = 1, 2, 4


def _moe_plan(gsel, tm):
    n = gsel.shape[0]
    nt = n // tm
    n_items = nt + N_EXPERT_GROUPS - 1
    perm = jnp.argsort(gsel, stable=True).astype(jnp.int32)
    gs = gsel[perm]
    gf, gl = gs[0::tm], gs[tm - 1::tm]
    grp = jnp.arange(N_EXPERT_GROUPS, dtype=jnp.int32)[None, :]
    active = ((grp >= gf[:, None]) & (grp <= gl[:, None])).reshape(-1)
    order = jnp.argsort(jnp.logical_not(active), stable=True).astype(jnp.int32)[:n_items]
    valid = active[order]
    last_real = order[jnp.sum(active.astype(jnp.int32)) - 1]
    item = jnp.where(valid, order, last_real)
    wt, wg = item // N_EXPERT_GROUPS, item % N_EXPERT_GROUPS
    flags = (jnp.where(valid, MOE_VALID, 0) + jnp.where(valid & (wg == gf[wt]), MOE_FIRST, 0)
             + jnp.where(valid & (wg == gl[wt]), MOE_LAST, 0)).astype(jnp.int32)
    return perm, wt.astype(jnp.int32), wg.astype(jnp.int32), flags


def _moe_body(perm_ref, wt_ref, wg_ref, fl_ref, h_hbm, wr_ref, br_ref, wgate_ref, wup_ref, wdown_ref, y_hbm,
              h32, hbf, acc, dws, sem, *, tm):
    w = pl.program_id(0)
    e = pl.program_id(1)
    flags = fl_ref[w]
    valid = (flags & MOE_VALID) != 0
    base = wt_ref[w] * tm

    def row_in(r):
        return pltpu.make_async_copy(h_hbm.at[pl.ds(perm_ref[base + r], 1), :], h32.at[pl.ds(r, 1), :], sem.at[0])

    def row_out(r):
        return pltpu.make_async_copy(acc.at[pl.ds(r, 1), :], y_hbm.at[pl.ds(perm_ref[base + r], 1), :], sem.at[1])

    def for_rows(fn):
        def body(r8, c):
            for s in range(8):
                fn(pl.multiple_of(r8 * 8, 8) + s)
            return c
        lax.fori_loop(0, tm // 8, body, 0)

    @pl.when(valid & ((flags & MOE_FIRST) != 0) & (e == 0))
    def _():
        for_rows(lambda r: row_in(r).start())
        for_rows(lambda r: row_in(r).wait())
        hb = h32[...].astype(BF16)
        hbf[...] = hb
        logits = jnp.dot(hb, wr_ref[...], preferred_element_type=F32) + br_ref[...]
        dws[...], _ = _route(logits)
        acc[...] = jnp.zeros(acc.shape, F32)

    @pl.when(valid)
    def _():
        h = hbf[...]
        a = jnp.dot(h, wgate_ref[...].astype(BF16), preferred_element_type=F32)
        up = jnp.dot(h, wup_ref[...].astype(BF16), preferred_element_type=F32)
        lane = lax.broadcasted_iota(jnp.int32, dws.shape, 1)
        ex = N_EXPERT_GROUPS + wg_ref[w] * EXPERTS_PER_GROUP + e
        wgt = jnp.sum(jnp.where(lane == ex, dws[...], 0.0), axis=-1, keepdims=True)
        act = ((_silu(a) * up) * wgt).astype(BF16)
        acc[...] += jnp.dot(act, wdown_ref[...].astype(BF16), preferred_element_type=F32)

    @pl.when(valid & ((flags & MOE_LAST) != 0) & (e == EXPERTS_PER_GROUP - 1))
    def _():
        for_rows(lambda r: row_out(r).start())
        for_rows(lambda r: row_out(r).wait())


def _moe_call(h2_all, grp_all, w_router, b_router, w_gate, w_up, w_down, layer, *, name):
    n, d = h2_all.shape
    f = w_gate.shape[-1]
    tm = 1024 if n % 1024 == 0 and n >= 2048 else 256
    perm, wt, wg, flags = _moe_plan(grp_all[:, 0].astype(jnp.int32), tm)
    n_items = wt.shape[0]

    def expert(w, e, perm_ref, wt_ref, wg_ref, fl_ref):
        e_eff = jnp.where((fl_ref[w] & MOE_VALID) != 0, e, EXPERTS_PER_GROUP - 1)
        return (layer, wg_ref[w] * EXPERTS_PER_GROUP + e_eff, 0, 0)

    return pl.pallas_call(
        functools.partial(_moe_body, tm=tm),
        grid_spec=pltpu.PrefetchScalarGridSpec(
            num_scalar_prefetch=4,
            grid=(n_items, EXPERTS_PER_GROUP),
            in_specs=[
                pl.BlockSpec(memory_space=pl.ANY),
                pl.BlockSpec((None, d, ROUTER_LANES), lambda w, e, *_: (layer, 0, 0)),
                pl.BlockSpec((None, 1, ROUTER_LANES), lambda w, e, *_: (layer, 0, 0)),
                pl.BlockSpec((None, None, d, f), expert),
                pl.BlockSpec((None, None, d, f), expert),
                pl.BlockSpec((None, None, f, d), expert),
            ],
            out_specs=pl.BlockSpec(memory_space=pl.ANY),
            scratch_shapes=[
                pltpu.VMEM((tm, d), F32),
                pltpu.VMEM((tm, d), BF16),
                pltpu.VMEM((tm, d), F32),
                pltpu.VMEM((tm, ROUTER_LANES), F32),
                pltpu.SemaphoreType.DMA((2,)),
            ],
        ),
        out_shape=jax.ShapeDtypeStruct((n, d), F32),
        compiler_params=_params(2, VMEM_LIMIT),
        name=name,
    )(perm, wt, wg, flags, h2_all, w_router, b_router, w_gate, w_up, w_down)


def _rope_tables(n_tok):
    rows = n_tok // GRID_W
    row = jnp.repeat(jnp.arange(rows, dtype=F32), GRID_W)
    col = jnp.tile(jnp.arange(GRID_W, dtype=F32), rows)
    n_freq = ROPE_DIM // 4
    inv = ROPE_BASE ** (-jnp.arange(n_freq, dtype=F32) / n_freq)
    ang = jnp.concatenate([row[:, None] * inv, col[:, None] * inv], axis=-1)
    cos, sin = jnp.cos(ang), jnp.sin(ang)
    return jnp.concatenate([cos, cos, cos, cos], axis=-1), jnp.concatenate([-sin, sin, -sin, sin], axis=-1)


def kernel(x_prompt, x_sample, cache_k, cache_v, state_ret, c, c_ctx, w_mod, b_mod, norm1, w_in, lambda_p, diff_norm, sg_norm, sg_w, sg_b, ret_decay, ret_norm, w_up_a, w_up_b, w_up_c, w_out, norm2, w_rg, b_rg, w_re, b_re, w_e_gate, w_e_up, w_e_down, final_norm):
    depth = w_in.shape[0]
    bp, tp, d = x_prompt.shape
    bs, ts, _ = x_sample.shape
    past = cache_k.shape[2]

    cond8 = jnp.zeros((8, d), F32).at[0].set(c_ctx).at[1:1 + bs].set(c)
    mods = _mod_call(cond8, w_mod, b_mod).reshape(depth, 8, N_MOD, d)
    mods = jnp.pad(mods, ((0, 0), (0, 0), (0, 8 - N_MOD), (0, 0)))

    wa_b, wb_b, wc_b, wo_b = (w.astype(BF16) for w in (w_up_a, w_up_b, w_up_c, w_out))
    w_router = jnp.concatenate(
        [w_rg, w_re, jnp.zeros((depth, d, ROUTER_LANES - N_EXPERT_GROUPS - N_EXPERTS), F32)], axis=-1).astype(BF16)
    b_router = jnp.concatenate(
        [b_rg, b_re, jnp.zeros((depth, ROUTER_LANES - N_EXPERT_GROUPS - N_EXPERTS), F32)], axis=-1)[:, None, :]
    log_g = jax.nn.log_sigmoid(ret_decay.astype(F32))
    sg_bt = jnp.swapaxes(sg_b, 1, 2)
    rope_tabs = _rope_tables(ts)
    cache_k2 = cache_k.reshape(bs, depth, past, W_QA)
    cache_v2 = cache_v.reshape(bs, depth, past, W_VA)

    groups = [
        dict(tag="p", x=x_prompt.reshape(bp * tp, d), nb=bp, t=tp, tt=bp * tp, rope=None, ctx=False, hb_a=8, hb_r=8),
        dict(tag="s", x=x_sample.reshape(bs * ts, d), nb=bs, t=ts, tt=ts, rope=rope_tabs, ctx=True, hb_a=1, hb_r=2),
    ]
    ks, vs, ss = [], [], []
    row0 = 0
    for g in groups:
        g["res"] = None
        g["row0"] = row0
        row0 += g["nb"] * g["t"]
    for l in range(depth):
        lam_init = 0.8 - 0.6 * math.exp(-0.3 * l)
        for g in groups:
            tag, nb, t, tt = g["tag"], g["nb"], g["t"], g["tt"]
            mod = mods[l, 0:1] if not g["ctx"] else mods[l, 1:1 + nb]
            x, h = _norm_call(g["x"], norm1[l], mod, g["res"], t_batch=tt, name=f"norm1_{tag}{l}")
            proj = _inproj_call(h, w_in, l, g["rope"], t_batch=tt, emit_kv=not g["ctx"], name=f"inproj_{tag}{l}")
            if not g["ctx"]:
                proj, ka_new, va_new = proj
            cache = (cache_k2, cache_v2) if g["ctx"] else None
            oa = _attn_call(proj, lambda_p, diff_norm[:, None, :], cache, l, n_batch=nb, t_batch=t,
                            lam_init=lam_init, hb=g["hb_a"], name=f"attn_{tag}{l}")
            o_f, o_b, s_fin = _ret_call(proj, log_g[l], state_ret if g["ctx"] else None, l, n_batch=nb, t_batch=t,
                                        hb=g["hb_r"], name=f"ret_{tag}{l}")
            merged = _mix_call(oa, o_f, o_b, proj, sg_norm[:, None, :], sg_w, sg_bt, ret_norm[:, None, :],
                               wa_b, wb_b, wc_b, l, t_batch=tt, name=f"mix_{tag}{l}")
            g["x"], g["mod"], g["merged"] = x, mod, merged
            if not g["ctx"]:
                ks.append(ka_new.reshape(nb, t, N_HEADS_A, 2, D_HEAD_A))
                vs.append(va_new.reshape(nb, t, N_HEADS_A, D_V_A))
                ss.append(s_fin)
        gp, gs_ = groups
        gp["x"], gs_["x"], h2_all, grp_all = _outproj_call(
            gp["merged"], gp["x"], gp["mod"], gs_["merged"], gs_["x"], gs_["mod"], wo_b, norm2[:, None, :],
            w_router, b_router, l, t_s=gs_["t"], name=f"outproj{l}")
        y_all = _moe_call(h2_all, grp_all, w_router, b_router, w_e_gate, w_e_up, w_e_down, l, name=f"moe{l}")
        for g in groups:
            g["res"] = (y_all, g["row0"], g["mod"])
    outs = []
    for g in groups:
        yn = _norm_call(g["x"], final_norm, None, g["res"], t_batch=g["tt"], name=f"final_{g['tag']}")
        outs.append(yn.reshape(g["nb"], g["t"], d))
    return (outs[0], outs[1], jnp.stack(ks, axis=1), jnp.stack(vs, axis=1), jnp.stack(ss, axis=1))
```

```python
import functools
import math

import jax
import jax.numpy as jnp
from jax import lax
from jax.experimental import pallas as pl
from jax.experimental.pallas import tpu as pltpu

F32 = jnp.float32
BF16 = jnp.bfloat16

GRID_W = 64
CHUNK = 128
N_HEADS_A = 8
D_HEAD_A = 64
D_V_A = 128
N_GROUPS_B = 8
D_GROUP_B = 128
N_HEADS_R = 8
D_K_R = 64
D_V_R = 128
ROPE_DIM = 64
ROPE_BASE = 10000.0
N_EXPERT_GROUPS = 4
EXPERTS_PER_GROUP = 4
N_EXPERTS = 16
D_EXPERT = 256
N_MOD = 6
EPS = 1e-6

W_QA = N_HEADS_A * 2 * D_HEAD_A
W_VA = N_HEADS_A * D_V_A
W_B = N_GROUPS_B * D_GROUP_B
W_QR = N_HEADS_R * D_K_R
W_VR = N_HEADS_R * D_V_R

OFF_QA = 0
OFF_KA = OFF_QA + W_QA
OFF_VA = OFF_KA + W_QA
OFF_U = OFF_VA + W_VA
OFF_V = OFF_U + W_B
OFF_QR = OFF_V + W_B
OFF_KR = OFF_QR + W_QR
OFF_VR = OFF_KR + W_QR
OFF_GR = OFF_VR + W_VR
OFF_GA = OFF_GR + W_VR

LANES = 128
RET_TILE = 256
ROUTER_LANES = 128
VMEM_LIMIT = 56 * 1024 * 1024


def _params(n_axes, vmem=None):
    return pltpu.CompilerParams(dimension_semantics=("arbitrary",) * n_axes, vmem_limit_bytes=vmem)


def _sigmoid(x):
    return 1.0 / (1.0 + jnp.exp(-x))


def _silu(x):
    return x * _sigmoid(x)


def _gelu_tanh(x):
    return 0.5 * x * (1.0 + jnp.tanh(math.sqrt(2.0 / math.pi) * (x + 0.044715 * (x * x * x))))


def _rms(x):
    return x * lax.rsqrt(jnp.mean(x * x, axis=-1, keepdims=True) + EPS)


def _mod_body(c_ref, w_ref, b_ref, o_ref):
    a = _silu(c_ref[...]).astype(BF16)
    o_ref[...] = jnp.dot(a, w_ref[...].astype(BF16), preferred_element_type=F32) + b_ref[...]


def _mod_call(cond8, w_mod, b_mod):
    depth, d, n6 = w_mod.shape
    tn = 1024
    return pl.pallas_call(
        _mod_body,
        grid=(depth, n6 // tn),
        in_specs=[
            pl.BlockSpec((8, d), lambda l, j: (0, 0)),
            pl.BlockSpec((None, d, tn), lambda l, j: (l, 0, j)),
            pl.BlockSpec((None, 1, tn), lambda l, j: (l, 0, j)),
        ],
        out_specs=pl.BlockSpec((None, 8, tn), lambda l, j: (l, 0, j)),
        out_shape=jax.ShapeDtypeStruct((depth, 8, n6), F32),
        compiler_params=_params(2, VMEM_LIMIT),
        name="mod_vectors",
    )(cond8, w_mod, b_mod.reshape(depth, 1, n6))


def _norm_body(*refs, has_res, adaln):
    it = iter(refs)
    x_ref = next(it)
    if has_res:
        y_ref = next(it)
        modp_ref = next(it)
    g_ref = next(it)
    if adaln:
        mod_ref = next(it)
    if has_res and adaln:
        xo_ref = next(it)
    h_ref = next(it)

    x = x_ref[...]
    if has_res:
        x = x + modp_ref[5:6, :] * y_ref[...]
        if adaln:
            xo_ref[...] = x
    y = _rms(x) * g_ref[...]
    if adaln:
        y = y * (1.0 + mod_ref[1:2, :]) + mod_ref[0:1, :]
    h_ref[...] = y.astype(h_ref.dtype)


def _norm_call(x, gain, mod=None, res=None, *, t_batch, name):
    n, d = x.shape
    tm = min(512, t_batch)
    per = t_batch // tm
    adaln = mod is not None
    has_res = res is not None
    row = pl.BlockSpec((tm, d), lambda i: (i, 0))
    modspec = pl.BlockSpec((None, 8, d), lambda i: (i // per, 0, 0))
    args, specs = [x], [row]
    if has_res:
        y_all, row0, mod_prev = res
        assert row0 % tm == 0
        args += [y_all, mod_prev]
        specs += [pl.BlockSpec((tm, d), lambda i: (i + row0 // tm, 0)), modspec]
    args.append(gain.reshape(1, d))
    specs.append(pl.BlockSpec((1, d), lambda i: (0, 0)))
    if adaln:
        args.append(mod)
        specs.append(modspec)
    out_shape, out_specs = [], []
    if has_res and adaln:
        out_shape.append(jax.ShapeDtypeStruct((n, d), F32))
        out_specs.append(row)
    out_shape.append(jax.ShapeDtypeStruct((n, d), BF16 if adaln else F32))
    out_specs.append(row)
    outs = pl.pallas_call(
        functools.partial(_norm_body, has_res=has_res, adaln=adaln),
        grid=(n // tm,),
        in_specs=specs,
        out_specs=out_specs,
        out_shape=out_shape,
        compiler_params=_params(1, VMEM_LIMIT),
        name=name,
    )(*args)
    if has_res and adaln:
        return outs[0], outs[1]
    return (x, outs[0]) if adaln else outs[0]


def _rope_swap(blk):
    lane = lax.broadcasted_iota(jnp.int32, blk.shape, 1)
    return jnp.where((lane & 32) == 0, pltpu.roll(blk, LANES - 32, 1), pltpu.roll(blk, 32, 1))


def _inproj_body(*refs, rope, tn):
    if rope:
        h_ref, w_ref, cs_ref, sn_ref, o_ref, wbf_ref = refs
    else:
        h_ref, w_ref, o_ref, wbf_ref = refs
    n = pl.program_id(0)
    m = pl.program_id(1)

    @pl.when(m == 0)
    def _():
        wbf_ref[...] = w_ref[...].astype(BF16)

    groups = tn // LANES
    qk_attn = n < (OFF_VA // tn)
    qk_ret = n == (OFF_QR // tn)

    def tile(rotate, scale_k):
        acc = jnp.dot(h_ref[...], wbf_ref[...], preferred_element_type=F32)
        if not rotate and not scale_k:
            o_ref[...] = acc
            return
        for j in range(groups):
            blk = acc[:, j * LANES:(j + 1) * LANES]
            if scale_k and j * LANES >= W_QR:
                blk = blk * (D_K_R ** -0.5)
            if rotate:
                blk = blk * cs_ref[...] + _rope_swap(blk) * sn_ref[...]
            o_ref[:, j * LANES:(j + 1) * LANES] = blk

    if rope:
        @pl.when(qk_attn)
        def _():
            tile(True, False)

    @pl.when(qk_ret)
    def _():
        tile(rope, True)

    @pl.when(jnp.logical_not(qk_ret | qk_attn) if rope else jnp.logical_not(qk_ret))
    def _():
        tile(False, False)


def _inproj_call(h, w_in, layer, rope_tabs, *, t_batch, name):
    n_tok, d = h.shape
    d_in = w_in.shape[-1]
    tn = 1024
    assert OFF_QR % tn == 0 and OFF_QR + 2 * W_QR == OFF_QR + tn and d_in % tn == 0
    tm = min(1024, t_batch)
    per = t_batch // tm
    m_tiles = n_tok // tm
    rope = rope_tabs is not None
    args = [h, w_in]
    specs = [
        pl.BlockSpec((tm, d), lambda n, m: (m, 0)),
        pl.BlockSpec((None, d, tn), lambda n, m: (layer, 0, n)),
    ]
    if rope:
        tab = pl.BlockSpec((tm, LANES), lambda n, m: (m % per, 0))
        args += list(rope_tabs)
        specs += [tab, tab]
    return pl.pallas_call(
        functools.partial(_inproj_body, rope=rope, tn=tn),
        grid=(d_in // tn, m_tiles),
        in_specs=specs,
        out_specs=pl.BlockSpec((tm, tn), lambda n, m: (m, n)),
        out_shape=jax.ShapeDtypeStruct((n_tok, d_in), F32),
        scratch_shapes=[pltpu.VMEM((d, tn), BF16)],
        compiler_params=_params(2, VMEM_LIMIT),
        name=name,
    )(*args)


SHIFT_SAFE = 40.0


def _attn_body(*refs, hb, t_new, past, lam_init, emit_kv, n_prev, bounded):
    it = iter(refs)
    lp_ref, q_ref, k_ref, v_ref = next(it), next(it), next(it), next(it)
    if past:
        ck_ref, cv_ref = next(it), next(it)
    g_ref = next(it)
    if emit_kv and n_prev:
        kp_ref, vp_ref = next(it), next(it)
    o_ref = next(it)
    if emit_kv:
        ko_ref, vo_ref = next(it), next(it)
    kbf, vbf = next(it), next(it)
    if bounded:
        kmax = next(it)
    i = pl.program_id(2)

    def map_mask(shape, mp):
        lane = lax.broadcasted_iota(jnp.int32, shape, 1)
        return (lane % LANES >= mp * D_HEAD_A) & (lane % LANES < (mp + 1) * D_HEAD_A)

    @pl.when(i == 0)
    def _():
        if past:
            kbf[0:past, :] = ck_ref[...].astype(BF16)
            vbf[0:past, :] = cv_ref[...].astype(BF16)
        kbf[past:past + t_new, :] = k_ref[...].astype(BF16)
        vbf[past:past + t_new, :] = v_ref[...].astype(BF16)
        if emit_kv:
            if n_prev:
                ko_ref[0:n_prev] = kp_ref[...]
                vo_ref[0:n_prev] = vp_ref[...]
            ko_ref[n_prev] = k_ref[...]
            vo_ref[n_prev] = v_ref[...]
        if bounded:
            k2 = kbf[...].astype(F32)
            k2 = k2 * k2
            for mp in range(2):
                n2 = jnp.sum(jnp.where(map_mask(k2.shape, mp), k2, 0.0), axis=-1, keepdims=True)
                kmax[mp] = jnp.broadcast_to(jnp.sqrt(jnp.max(n2, axis=0, keepdims=True)), kmax.shape[1:])

    lp = lp_ref[...]
    lam = (jnp.exp(jnp.sum(lp[0:1, :] * lp[1:2, :], keepdims=True))
           - jnp.exp(jnp.sum(lp[2:3, :] * lp[3:4, :], keepdims=True)) + lam_init)
    nt = (((1,), (1,)), ((), ()))

    def head(j, shifts):
        sl = slice(j * LANES, (j + 1) * LANES)
        k = kbf[:, sl]
        ps, rs = [], []
        for mp in range(2):
            s = lax.dot_general(qms[j][mp], k, nt, preferred_element_type=F32)
            c = jnp.max(s, axis=-1, keepdims=True) if shifts is None else shifts[mp]
            p = jnp.exp(s - c)
            ps.append(p)
            rs.append(jnp.sum(p, axis=-1, keepdims=True))
        a = (ps[0] * (1.0 / rs[0]) - ps[1] * (lam / rs[1])).astype(BF16)
        o = jnp.dot(a, vbf[:, sl], preferred_element_type=F32)
        o_ref[:, sl] = ((_rms(o) * g_ref[...]) * (1.0 - lam_init)).astype(o_ref.dtype)

    qms = []
    for j in range(hb):
        q = q_ref[:, j * LANES:(j + 1) * LANES] * (D_HEAD_A ** -0.5)
        qms.append([jnp.where(map_mask(q.shape, mp), q, 0.0).astype(BF16) for mp in range(2)])

    if not bounded:
        for j in range(hb):
            head(j, None)
        return

    assert hb == 1
    shifts = []
    for mp in range(2):
        qf = qms[0][mp].astype(F32)
        qn = jnp.sqrt(jnp.sum(qf * qf, axis=-1, keepdims=True))
        shifts.append(qn * kmax[mp][0:1, 0:1] * 1.001 + 1e-6)
    safe = jnp.max(jnp.maximum(shifts[0], shifts[1])) <= SHIFT_SAFE

    @pl.when(safe)
    def _():
        head(0, shifts)

    @pl.when(jnp.logical_not(safe))
    def _():
        head(0, None)


def _attn_call(proj, lambda_p, diff_norm, cache, layer, *, n_batch, t_batch, lam_init, hb, emit_kv, kv_prev, name):
    n_tok = proj.shape[0]
    tq = min(256, t_batch)
    qsteps = t_batch // tq
    w = hb * LANES
    past = 0 if cache is None else cache[0].shape[2]
    assert not emit_kv or (w == W_QA and qsteps == 1)
    args = [lambda_p, proj, proj, proj]
    specs = [
        pl.BlockSpec((None, 4, D_HEAD_A), lambda b, h, i: (layer, 0, 0)),
        pl.BlockSpec((tq, w), lambda b, h, i: (b * qsteps + i, OFF_QA // w + h)),
        pl.BlockSpec((t_batch, w), lambda b, h, i: (b, OFF_KA // w + h)),
        pl.BlockSpec((t_batch, w), lambda b, h, i: (b, OFF_VA // w + h)),
    ]
    if past:
        cspec = pl.BlockSpec((None, None, past, w), lambda b, h, i: (b, layer, 0, h))
        args += [cache[0], cache[1]]
        specs += [cspec, cspec]
    args.append(diff_norm)
    specs.append(pl.BlockSpec((None, 1, D_V_A), lambda b, h, i: (layer, 0, 0)))
    t_keys = past + t_batch
    out_specs = [pl.BlockSpec((tq, w), lambda b, h, i: (b * qsteps + i, h))]
    out_shape = [jax.ShapeDtypeStruct((n_tok, W_VA), BF16)]
    n_prev = 0
    if emit_kv:
        n_prev = 0 if kv_prev is None else kv_prev[0].shape[1]
        if n_prev:
            args += list(kv_prev)
            specs += [pl.BlockSpec((None, n_prev, t_batch, w), lambda b, h, i: (b, 0, 0, 0))] * 2
        out_specs += [pl.BlockSpec((None, n_prev + 1, t_batch, w), lambda b, h, i: (b, 0, 0, 0))] * 2
        out_shape += [jax.ShapeDtypeStruct((n_batch, n_prev + 1, t_batch, w), F32)] * 2
    bounded = hb == 1
    scratch = [pltpu.VMEM((t_keys, w), BF16), pltpu.VMEM((t_keys, w), BF16)]
    if bounded:
        scratch.append(pltpu.VMEM((2, 8, LANES), F32))
    outs = pl.pallas_call(
        functools.partial(_attn_body, hb=hb, t_new=t_batch, past=past, lam_init=lam_init, emit_kv=emit_kv,
                          n_prev=n_prev, bounded=bounded),
        grid=(n_batch, N_HEADS_A // hb, qsteps),
        in_specs=specs,
        out_specs=out_specs,
        out_shape=out_shape,
        scratch_shapes=scratch,
        compiler_params=_params(3, VMEM_LIMIT),
        name=name,
    )(*args)
    return outs if emit_kv else outs[0]


def _ret_body(*refs, hb, nc, c_, has_state):
    if has_state:
        lg_ref, q_ref, k_ref, v_ref, s0_ref, of_ref, ob_ref, sf_ref, st_ref, dec_ref, vec_ref, cd_ref = refs
    else:
        lg_ref, q_ref, k_ref, v_ref, of_ref, ob_ref, sf_ref, st_ref, dec_ref, vec_ref, cd_ref = refs
    hblk = pl.program_id(1)
    half = D_K_R

    rel = (lax.broadcasted_iota(jnp.int32, (c_, c_), 0) - lax.broadcasted_iota(jnp.int32, (c_, c_), 1)).astype(F32)
    pos = lax.broadcasted_iota(jnp.int32, (c_, LANES), 0).astype(F32)
    for j in range(hb):
        for d in range(2):
            lg = lg_ref[d, hblk * hb + j]
            if d == 0:
                dec_ref[d, j] = jnp.where(rel >= 0, jnp.exp(jnp.maximum(rel, 0.0) * lg), 0.0)
                vec_ref[d, j, 0] = jnp.exp((pos + 1.0) * lg)
                vec_ref[d, j, 1] = jnp.exp((c_ - 1.0 - pos) * lg)
            else:
                dec_ref[d, j] = jnp.where(rel <= 0, jnp.exp(jnp.maximum(-rel, 0.0) * lg), 0.0)
                vec_ref[d, j, 0] = jnp.exp((c_ - pos) * lg)
                vec_ref[d, j, 1] = jnp.exp(pos * lg)
            cd_ref[d, j] = jnp.exp(jnp.full((LANES, D_V_R), float(c_), F32) * lg)
            st_ref[d, j] = jnp.zeros((LANES, D_V_R), F32)
            if has_state:
                lo = (j % 2) * half
                st_ref[d, j, lo:lo + half, :] = s0_ref[d, j]

    nt = (((1,), (1,)), ((), ()))

    def step(i, carry):
        for d in range(2):
            c = i if d == 0 else nc - 1 - i
            r0 = pl.multiple_of(c * c_, c_)
            for j in range(hb):
                jp = j // 2
                q = q_ref[pl.ds(r0, c_), jp * LANES:(jp + 1) * LANES]
                k = k_ref[pl.ds(r0, c_), jp * LANES:(jp + 1) * LANES]
                v = v_ref[pl.ds(r0, c_), j * D_V_R:(j + 1) * D_V_R].astype(BF16)
                lane = lax.broadcasted_iota(jnp.int32, q.shape, 1)
                mine = (lane >= (j % 2) * half) & (lane < (j % 2 + 1) * half)
                qm = jnp.where(mine, q, 0.0).astype(BF16)
                s = lax.dot_general(qm, k.astype(BF16), nt, preferred_element_type=F32) * dec_ref[d, j]
                slab = st_ref[d, j]
                o = (jnp.dot(s.astype(BF16), v, preferred_element_type=F32)
                     + jnp.dot(qm, slab.astype(BF16), preferred_element_type=F32) * vec_ref[d, j, 0])
                kzt = (k * vec_ref[d, j, 1]).T.astype(BF16)
                st_ref[d, j] = slab * cd_ref[d, j] + jnp.dot(kzt, v, preferred_element_type=F32)
                if d == 0:
                    of_ref[pl.ds(r0, c_), j * D_V_R:(j + 1) * D_V_R] = o
                else:
                    ob_ref[pl.ds(r0, c_), j * D_V_R:(j + 1) * D_V_R] = o
        return carry

    lax.fori_loop(0, nc, step, 0, unroll=4 if nc % 4 == 0 else 1)
    for j in range(hb):
        lo = (j % 2) * half
        for d in range(2):
            sf_ref[d, j] = st_ref[d, j, lo:lo + half, :]


def _ret_call(proj, log_g, state, layer, *, n_batch, t_batch, hb, name):
    n_tok = proj.shape[0]
    c_ = min(RET_TILE, t_batch)
    nc = t_batch // c_
    wq, wv = hb * D_K_R, hb * D_V_R
    args = [log_g, proj, proj, proj]
    specs = [
        pl.BlockSpec(memory_space=pltpu.SMEM),
        pl.BlockSpec((t_batch, wq), lambda b, h: (b, OFF_QR // wq + h)),
        pl.BlockSpec((t_batch, wq), lambda b, h: (b, OFF_KR // wq + h)),
        pl.BlockSpec((t_batch, wv), lambda b, h: (b, OFF_VR // wv + h)),
    ]
    has_state = state is not None
    if has_state:
        args.append(state)
        specs.append(pl.BlockSpec((None, None, 2, hb, D_K_R, D_V_R), lambda b, h: (b, layer, 0, h, 0, 0)))
    ospec = pl.BlockSpec((t_batch, wv), lambda b, h: (b, h))
    return pl.pallas_call(
        functools.partial(_ret_body, hb=hb, nc=nc, c_=c_, has_state=has_state),
        grid=(n_batch, N_HEADS_R // hb),
        in_specs=specs,
        out_specs=[ospec, ospec, pl.BlockSpec((None, 2, hb, D_K_R, D_V_R), lambda b, h: (b, 0, h, 0, 0))],
        out_shape=[
            jax.ShapeDtypeStruct((n_tok, W_VR), F32),
            jax.ShapeDtypeStruct((n_tok, W_VR), F32),
            jax.ShapeDtypeStruct((n_batch, 2, N_HEADS_R, D_K_R, D_V_R), F32),
        ],
        scratch_shapes=[
            pltpu.VMEM((2, hb, LANES, D_V_R), F32),
            pltpu.VMEM((2, hb, c_, c_), F32),
            pltpu.VMEM((2, hb, 2, c_, LANES), F32),
            pltpu.VMEM((2, hb, LANES, D_V_R), F32),
        ],
        compiler_params=_params(2, VMEM_LIMIT),
        name=name,
    )(*args)


def _mix_body(oa_ref, of_ref, ob_ref, u_ref, v_ref, gr_ref, ga_ref, gb_ref, gc_ref, sgn_ref, sgw_ref, sgb_ref,
              rn_ref, wa_ref, wb_ref, wc_ref, o_ref, sb_s, oc_s, *, tm):
    for j in range(N_HEADS_R):
        sl = slice(j * D_V_R, (j + 1) * D_V_R)
        y = _rms(of_ref[:, sl] + ob_ref[:, sl]) * rn_ref[:, sl]
        oc_s[:, sl] = (_silu(gr_ref[:, sl]) * y).astype(BF16)
    gv = _gelu_tanh(v_ref[...])
    vn = (_rms(gv) * sgn_ref[...]).astype(BF16)
    for g in range(N_GROUPS_B):
        sl = slice(g * D_GROUP_B, (g + 1) * D_GROUP_B)
        wg = sgw_ref[g].astype(BF16)
        bias = sgb_ref[:, g:g + 1]
        for c in range(tm // CHUNK):
            rows = slice(c * CHUNK, (c + 1) * CHUNK)
            mixed = jnp.dot(wg, vn[rows, sl], preferred_element_type=F32) + bias
            sb_s[rows, sl] = (_gelu_tanh(u_ref[rows, sl]) * mixed).astype(BF16)

    ya = jnp.dot(oa_ref[...], wa_ref[...], preferred_element_type=F32)
    yb = jnp.dot(sb_s[...], wb_ref[...], preferred_element_type=F32)
    yc = jnp.dot(oc_s[...], wc_ref[...], preferred_element_type=F32)
    merged = _sigmoid(ga_ref[...]) * ya + _sigmoid(gb_ref[...]) * yb + _sigmoid(gc_ref[...]) * yc
    o_ref[...] = merged.astype(o_ref.dtype)


def _mix_call(oa, o_f, o_b, proj, sg_norm, sg_w, sg_bt, ret_norm, wa, wb, wc, layer, *, t_batch, name):
    n_tok = oa.shape[0]
    d = wa.shape[-1]
    tm = min(256, t_batch)
    wide = pl.BlockSpec((tm, W_B), lambda m: (m, 0))

    def pcol(off, width):
        return pl.BlockSpec((tm, width), lambda m: (m, off // width))

    def lvec(width):
        return pl.BlockSpec((None, 1, width), lambda m: (layer, 0, 0))

    wspec = pl.BlockSpec((None, W_B, d), lambda m: (layer, 0, 0), pipeline_mode=pl.Buffered(1))
    return pl.pallas_call(
        functools.partial(_mix_body, tm=tm),
        grid=(n_tok // tm,),
        in_specs=[
            wide, wide, wide, pcol(OFF_U, W_B), pcol(OFF_V, W_B), pcol(OFF_GR, W_VR),
            pcol(OFF_GA, d), pcol(OFF_GA + d, d), pcol(OFF_GA + 2 * d, d),
            lvec(W_B),
            pl.BlockSpec((None, N_GROUPS_B, CHUNK, CHUNK), lambda m: (layer, 0, 0, 0)),
            pl.BlockSpec((None, CHUNK, N_GROUPS_B), lambda m: (layer, 0, 0)),
            lvec(W_VR), wspec, wspec, wspec,
        ],
        out_specs=pl.BlockSpec((tm, d), lambda m: (m, 0)),
        out_shape=jax.ShapeDtypeStruct((n_tok, d), BF16),
        scratch_shapes=[pltpu.VMEM((tm, W_B), BF16), pltpu.VMEM((tm, W_VR), BF16)],
        compiler_params=_params(1, VMEM_LIMIT),
        name=name,
    )(oa, o_f, o_b, proj, proj, proj, proj, proj, proj, sg_norm, sg_w, sg_bt, ret_norm, wa, wb, wc)


def _route(logits):
    lane = lax.broadcasted_iota(jnp.int32, logits.shape, 1).astype(F32)
    big = float(1 << 20)
    neg = -jnp.inf
    lgm = jnp.where(lane < N_EXPERT_GROUPS, logits, neg)
    mx = jnp.max(lgm, axis=-1, keepdims=True)
    p_group = 1.0 / jnp.sum(jnp.exp(lgm - mx), axis=-1, keepdims=True)
    gsel = jnp.min(jnp.where(lgm == mx, lane, big), axis=-1, keepdims=True)
    lo = N_EXPERT_GROUPS + gsel * EXPERTS_PER_GROUP
    insel = (lane >= lo) & (lane < lo + EXPERTS_PER_GROUP)
    le = jnp.where(insel, logits, neg)
    v1 = jnp.max(le, axis=-1, keepdims=True)
    i1 = jnp.min(jnp.where(le == v1, lane, big), axis=-1, keepdims=True)
    le2 = jnp.where(lane == i1, neg, le)
    v2 = jnp.max(le2, axis=-1, keepdims=True)
    i2 = jnp.min(jnp.where(le2 == v2, lane, big), axis=-1, keepdims=True)
    e2 = jnp.exp(v2 - v1)
    den = 1.0 + e2
    w1 = p_group * (1.0 / den)
    w2 = p_group * (e2 / den)
    return jnp.where(lane == i1, w1, 0.0) + jnp.where(lane == i2, w2, 0.0), gsel


def _outproj_body(mg_p, x_p, mod_p, mg_s, x_s, mod_s, w_ref, g_ref, wr_ref, br_ref, x1_p, x1_s, h2_ref, rt_ref, *,
                  steps_p):
    def run(mg_ref, x_ref, mod_ref, x1_ref):
        y = jnp.dot(mg_ref[...], w_ref[...], preferred_element_type=F32)
        x1 = x_ref[...] + mod_ref[2:3, :] * y
        x1_ref[...] = x1
        h2 = (_rms(x1) * g_ref[...]) * (1.0 + mod_ref[4:5, :]) + mod_ref[3:4, :]
        h2_ref[...] = h2
        logits = jnp.dot(h2.astype(BF16), wr_ref[...], preferred_element_type=F32) + br_ref[...]
        _, gsel = _route(logits)
        rt_ref[...] = jnp.broadcast_to(gsel, rt_ref.shape)

    i = pl.program_id(0)

    @pl.when(i < steps_p)
    def _():
        run(mg_p, x_p, mod_p, x1_p)

    @pl.when(i >= steps_p)
    def _():
        run(mg_s, x_s, mod_s, x1_s)


def _outproj_call(merged_p, x_p, mod_p, merged_s, x_s, mod_s, w_out, norm2, w_router, b_router, layer, *, t_s, name):
    n_p, d = x_p.shape
    n_s = x_s.shape[0]
    tm = 256
    assert n_p % tm == 0 and t_s % tm == 0
    sp, ss = n_p // tm, n_s // tm
    per_s = t_s // tm
    row_p = pl.BlockSpec((tm, d), lambda i: (jnp.minimum(i, sp - 1), 0))
    row_s = pl.BlockSpec((tm, d), lambda i: (jnp.maximum(i - sp, 0), 0))
    return pl.pallas_call(
        functools.partial(_outproj_body, steps_p=sp),
        grid=(sp + ss,),
        in_specs=[
            row_p, row_p, pl.BlockSpec((None, 8, d), lambda i: (0, 0, 0)),
            row_s, row_s, pl.BlockSpec((None, 8, d), lambda i: (jnp.maximum(i - sp, 0) // per_s, 0, 0)),
            pl.BlockSpec((None, d, d), lambda i: (layer, 0, 0)),
            pl.BlockSpec((None, 1, d), lambda i: (layer, 0, 0)),
            pl.BlockSpec((None, d, ROUTER_LANES), lambda i: (layer, 0, 0)),
            pl.BlockSpec((None, 1, ROUTER_LANES), lambda i: (layer, 0, 0)),
        ],
        out_specs=[row_p, row_s, pl.BlockSpec((tm, d), lambda i: (i, 0)),
                   pl.BlockSpec((tm, ROUTER_LANES), lambda i: (i, 0))],
        out_shape=[
            jax.ShapeDtypeStruct((n_p, d), F32),
            jax.ShapeDtypeStruct((n_s, d), F32),
            jax.ShapeDtypeStruct((n_p + n_s, d), F32),
            jax.ShapeDtypeStruct((n_p + n_s, ROUTER_LANES), F32),
        ],
        compiler_params=_params(1, VMEM_LIMIT),
        name=name,
    )(merged_p, x_p, mod_p, merged_s, x_s, mod_s, w_out, norm2, w_router, b_router)


MOE_FIRST, MOE_LAST, MOE_VALID = 1, 2, 4


def _moe_plan(gsel, tm):
    n = gsel.shape[0]
    nt = n // tm
    n_items = nt + N_EXPERT_GROUPS - 1
    perm = jnp.argsort(gsel, stable=True).astype(jnp.int32)
    gs = gsel[perm]
    gf, gl = gs[0::tm], gs[tm - 1::tm]
    grp = jnp.arange(N_EXPERT_GROUPS, dtype=jnp.int32)[None, :]
    active = ((grp >= gf[:, None]) & (grp <= gl[:, None])).reshape(-1)
    order = jnp.argsort(jnp.logical_not(active), stable=True).astype(jnp.int32)[:n_items]
    valid = active[order]
    last_real = order[jnp.sum(active.astype(jnp.int32)) - 1]
    item = jnp.where(valid, order, last_real)
    wt, wg = item // N_EXPERT_GROUPS, item % N_EXPERT_GROUPS
    flags = (jnp.where(valid, MOE_VALID, 0) + jnp.where(valid & (wg == gf[wt]), MOE_FIRST, 0)
             + jnp.where(valid & (wg == gl[wt]), MOE_LAST, 0)).astype(jnp.int32)
    return perm, wt.astype(jnp.int32), wg.astype(jnp.int32), flags


def _moe_body(perm_ref, wt_ref, wg_ref, fl_ref, h_hbm, wr_ref, br_ref, wgate_ref, wup_ref, wdown_ref, y_hbm,
              h32, hbf, acc, dws, sem, *, tm):
    w = pl.program_id(0)
    e = pl.program_id(1)
    flags = fl_ref[w]
    valid = (flags & MOE_VALID) != 0
    base = wt_ref[w] * tm

    def row_in(r):
        return pltpu.make_async_copy(h_hbm.at[pl.ds(perm_ref[base + r], 1), :], h32.at[pl.ds(r, 1), :], sem.at[0])

    def row_out(r):
        return pltpu.make_async_copy(acc.at[pl.ds(r, 1), :], y_hbm.at[pl.ds(perm_ref[base + r], 1), :], sem.at[1])

    def for_rows(fn):
        def body(r8, c):
            for s in range(8):
                fn(pl.multiple_of(r8 * 8, 8) + s)
            return c
        lax.fori_loop(0, tm // 8, body, 0)

    @pl.when(valid & ((flags & MOE_FIRST) != 0) & (e == 0))
    def _():
        for_rows(lambda r: row_in(r).start())
        for_rows(lambda r: row_in(r).wait())
        hb = h32[...].astype(BF16)
        hbf[...] = hb
        logits = jnp.dot(hb, wr_ref[...], preferred_element_type=F32) + br_ref[...]
        dws[...], _ = _route(logits)
        acc[...] = jnp.zeros(acc.shape, F32)

    @pl.when(valid)
    def _():
        h = hbf[...]
        a = jnp.dot(h, wgate_ref[...].astype(BF16), preferred_element_type=F32)
        up = jnp.dot(h, wup_ref[...].astype(BF16), preferred_element_type=F32)
        lane = lax.broadcasted_iota(jnp.int32, dws.shape, 1)
        ex = N_EXPERT_GROUPS + wg_ref[w] * EXPERTS_PER_GROUP + e
        wgt = jnp.sum(jnp.where(lane == ex, dws[...], 0.0), axis=-1, keepdims=True)
        act = ((_silu(a) * up) * wgt).astype(BF16)
        acc[...] += jnp.dot(act, wdown_ref[...].astype(BF16), preferred_element_type=F32)

    @pl.when(valid & ((flags & MOE_LAST) != 0) & (e == EXPERTS_PER_GROUP - 1))
    def _():
        for_rows(lambda r: row_out(r).start())
        for_rows(lambda r: row_out(r).wait())


def _moe_call(h2_all, grp_all, w_router, b_router, w_gate, w_up, w_down, layer, *, name):
    n, d = h2_all.shape
    f = w_gate.shape[-1]
    tm = 1024 if n % 1024 == 0 and n >= 2048 else 256
    perm, wt, wg, flags = _moe_plan(grp_all[:, 0].astype(jnp.int32), tm)
    n_items = wt.shape[0]

    def expert(w, e, perm_ref, wt_ref, wg_ref, fl_ref):
        e_eff = jnp.where((fl_ref[w] & MOE_VALID) != 0, e, EXPERTS_PER_GROUP - 1)
        return (layer, wg_ref[w] * EXPERTS_PER_GROUP + e_eff, 0, 0)

    return pl.pallas_call(
        functools.partial(_moe_body, tm=tm),
        grid_spec=pltpu.PrefetchScalarGridSpec(
            num_scalar_prefetch=4,
            grid=(n_items, EXPERTS_PER_GROUP),
            in_specs=[
                pl.BlockSpec(memory_space=pl.ANY),
                pl.BlockSpec((None, d, ROUTER_LANES), lambda w, e, *_: (layer, 0, 0)),
                pl.BlockSpec((None, 1, ROUTER_LANES), lambda w, e, *_: (layer, 0, 0)),
                pl.BlockSpec((None, None, d, f), expert),
                pl.BlockSpec((None, None, d, f), expert),
                pl.BlockSpec((None, None, f, d), expert),
            ],
            out_specs=pl.BlockSpec(memory_space=pl.ANY),
            scratch_shapes=[
                pltpu.VMEM((tm, d), F32),
                pltpu.VMEM((tm, d), BF16),
                pltpu.VMEM((tm, d), F32),
                pltpu.VMEM((tm, ROUTER_LANES), F32),
                pltpu.SemaphoreType.DMA((2,)),
            ],
        ),
        out_shape=jax.ShapeDtypeStruct((n, d), F32),
        compiler_params=_params(2, VMEM_LIMIT),
        name=name,
    )(perm, wt, wg, flags, h2_all, w_router, b_router, w_gate, w_up, w_down)


def _rope_tables(n_tok):
    rows = n_tok // GRID_W
    row = jnp.repeat(jnp.arange(rows, dtype=F32), GRID_W)
    col = jnp.tile(jnp.arange(GRID_W, dtype=F32), rows)
    n_freq = ROPE_DIM // 4
    inv = ROPE_BASE ** (-jnp.arange(n_freq, dtype=F32) / n_freq)
    ang = jnp.concatenate([row[:, None] * inv, col[:, None] * inv], axis=-1)
    cos, sin = jnp.cos(ang), jnp.sin(ang)
    return jnp.concatenate([cos, cos, cos, cos], axis=-1), jnp.concatenate([-sin, sin, -sin, sin], axis=-1)


def kernel(x_prompt, x_sample, cache_k, cache_v, state_ret, c, c_ctx, w_mod, b_mod, norm1, w_in, lambda_p, diff_norm, sg_norm, sg_w, sg_b, ret_decay, ret_norm, w_up_a, w_up_b, w_up_c, w_out, norm2, w_rg, b_rg, w_re, b_re, w_e_gate, w_e_up, w_e_down, final_norm):
    depth = w_in.shape[0]
    bp, tp, d = x_prompt.shape
    bs, ts, _ = x_sample.shape
    past = cache_k.shape[2]

    cond8 = jnp.zeros((8, d), F32).at[0].set(c_ctx).at[1:1 + bs].set(c)
    mods = _mod_call(cond8, w_mod, b_mod).reshape(depth, 8, N_MOD, d)
    mods = jnp.pad(mods, ((0, 0), (0, 0), (0, 8 - N_MOD), (0, 0)))

    wa_b, wb_b, wc_b, wo_b = (w.astype(BF16) for w in (w_up_a, w_up_b, w_up_c, w_out))
    w_router = jnp.concatenate(
        [w_rg, w_re, jnp.zeros((depth, d, ROUTER_LANES - N_EXPERT_GROUPS - N_EXPERTS), F32)], axis=-1).astype(BF16)
    b_router = jnp.concatenate(
        [b_rg, b_re, jnp.zeros((depth, ROUTER_LANES - N_EXPERT_GROUPS - N_EXPERTS), F32)], axis=-1)[:, None, :]
    log_g = jax.nn.log_sigmoid(ret_decay.astype(F32))
    sg_bt = jnp.swapaxes(sg_b, 1, 2)
    rope_tabs = _rope_tables(ts)
    cache_k2 = cache_k.reshape(bs, depth, past, W_QA)
    cache_v2 = cache_v.reshape(bs, depth, past, W_VA)

    groups = [
        dict(tag="p", x=x_prompt.reshape(bp * tp, d), nb=bp, t=tp, tt=bp * tp, rope=None, ctx=False, hb_a=8, hb_r=8),
        dict(tag="s", x=x_sample.reshape(bs * ts, d), nb=bs, t=ts, tt=ts, rope=rope_tabs, ctx=True, hb_a=1, hb_r=2),
    ]
    kv_new, ss = None, []
    row0 = 0
    for g in groups:
        g["res"] = None
        g["row0"] = row0
        row0 += g["nb"] * g["t"]
    for l in range(depth):
        lam_init = 0.8 - 0.6 * math.exp(-0.3 * l)
        for g in groups:
            tag, nb, t, tt = g["tag"], g["nb"], g["t"], g["tt"]
            mod = mods[l, 0:1] if not g["ctx"] else mods[l, 1:1 + nb]
            x, h = _norm_call(g["x"], norm1[l], mod, g["res"], t_batch=tt, name=f"norm1_{tag}{l}")
            proj = _inproj_call(h, w_in, l, g["rope"], t_batch=tt, name=f"inproj_{tag}{l}")
            cache = (cache_k2, cache_v2) if g["ctx"] else None
            oa = _attn_call(proj, lambda_p, diff_norm[:, None, :], cache, l, n_batch=nb, t_batch=t,
                            lam_init=lam_init, hb=g["hb_a"], emit_kv=not g["ctx"], kv_prev=kv_new,
                            name=f"attn_{tag}{l}")
            if not g["ctx"]:
                oa, kv_new = oa[0], (oa[1], oa[2])
            o_f, o_b, s_fin = _ret_call(proj, log_g[l], state_ret if g["ctx"] else None, l, n_batch=nb, t_batch=t,
                                        hb=g["hb_r"], name=f"ret_{tag}{l}")
            merged = _mix_call(oa, o_f, o_b, proj, sg_norm[:, None, :], sg_w, sg_bt, ret_norm[:, None, :],
                               wa_b, wb_b, wc_b, l, t_batch=tt, name=f"mix_{tag}{l}")
            g["x"], g["mod"], g["merged"] = x, mod, merged
            if not g["ctx"]:
                ss.append(s_fin)
        gp, gs_ = groups
        gp["x"], gs_["x"], h2_all, grp_all = _outproj_call(
            gp["merged"], gp["x"], gp["mod"], gs_["merged"], gs_["x"], gs_["mod"], wo_b, norm2[:, None, :],
            w_router, b_router, l, t_s=gs_["t"], name=f"outproj{l}")
        y_all = _moe_call(h2_all, grp_all, w_router, b_router, w_e_gate, w_e_up, w_e_down, l, name=f"moe{l}")
        for g in groups:
            g["res"] = (y_all, g["row0"], g["mod"])
    outs = []
    for g in groups:
        yn = _norm_call(g["x"], final_norm, None, g["res"], t_batch=g["tt"], name=f"final_{g['tag']}")
        outs.append(yn.reshape(g["nb"], g["t"], d))
    new_k = kv_new[0].reshape(bp, depth, tp, N_HEADS_A, 2, D_HEAD_A)
    new_v = kv_new[1].reshape(bp, depth, tp, N_HEADS_A, D_V_A)
    return (outs[0], outs[1], new_k, new_v, jnp.stack(ss, axis=1))
```

```python
import functools
import math

import jax
import jax.numpy as jnp
from jax import lax
from jax.experimental import pallas as pl
from jax.experimental.pallas import tpu as pltpu

F32 = jnp.float32
BF16 = jnp.bfloat16

GRID_W = 64
CHUNK = 128
N_HEADS_A = 8
D_HEAD_A = 64
D_V_A = 128
N_GROUPS_B = 8
D_GROUP_B = 128
N_HEADS_R = 8
D_K_R = 64
D_V_R = 128
ROPE_DIM = 64
ROPE_BASE = 10000.0
N_EXPERT_GROUPS = 4
EXPERTS_PER_GROUP = 4
N_EXPERTS = 16
D_EXPERT = 256
N_MOD = 6
EPS = 1e-6

W_QA = N_HEADS_A * 2 * D_HEAD_A
W_VA = N_HEADS_A * D_V_A
W_B = N_GROUPS_B * D_GROUP_B
W_QR = N_HEADS_R * D_K_R
W_VR = N_HEADS_R * D_V_R

OFF_QA = 0
OFF_KA = OFF_QA + W_QA
OFF_VA = OFF_KA + W_QA
OFF_U = OFF_VA + W_VA
OFF_V = OFF_U + W_B
OFF_QR = OFF_V + W_B
OFF_KR = OFF_QR + W_QR
OFF_VR = OFF_KR + W_QR
OFF_GR = OFF_VR + W_VR
OFF_GA = OFF_GR + W_VR

LANES = 128
RET_TILE = 256
ROUTER_LANES = 128
VMEM_LIMIT = 56 * 1024 * 1024


def _params(n_axes, vmem=None):
    return pltpu.CompilerParams(dimension_semantics=("arbitrary",) * n_axes, vmem_limit_bytes=vmem)


def _sigmoid(x):
    return 1.0 / (1.0 + jnp.exp(-x))


def _silu(x):
    return x * _sigmoid(x)


def _gelu_tanh(x):
    return 0.5 * x * (1.0 + jnp.tanh(math.sqrt(2.0 / math.pi) * (x + 0.044715 * (x * x * x))))


def _rms(x):
    return x * lax.rsqrt(jnp.mean(x * x, axis=-1, keepdims=True) + EPS)


def _mod_body(c_ref, w_ref, b_ref, o_ref):
    a = _silu(c_ref[...]).astype(BF16)
    o_ref[...] = jnp.dot(a, w_ref[...].astype(BF16), preferred_element_type=F32) + b_ref[...]


def _mod_call(cond8, w_mod, b_mod):
    depth, d, n6 = w_mod.shape
    tn = 1024
    return pl.pallas_call(
        _mod_body,
        grid=(depth, n6 // tn),
        in_specs=[
            pl.BlockSpec((8, d), lambda l, j: (0, 0)),
            pl.BlockSpec((None, d, tn), lambda l, j: (l, 0, j)),
            pl.BlockSpec((None, 1, tn), lambda l, j: (l, 0, j)),
        ],
        out_specs=pl.BlockSpec((None, 8, tn), lambda l, j: (l, 0, j)),
        out_shape=jax.ShapeDtypeStruct((depth, 8, n6), F32),
        compiler_params=_params(2, VMEM_LIMIT),
        name="mod_vectors",
    )(cond8, w_mod, b_mod.reshape(depth, 1, n6))


def _norm_body(*refs, has_res, adaln):
    it = iter(refs)
    x_ref = next(it)
    if has_res:
        y_ref = next(it)
        modp_ref = next(it)
    g_ref = next(it)
    if adaln:
        mod_ref = next(it)
    if has_res and adaln:
        xo_ref = next(it)
    h_ref = next(it)

    x = x_ref[...]
    if has_res:
        x = x + modp_ref[5:6, :] * y_ref[...]
        if adaln:
            xo_ref[...] = x
    y = _rms(x) * g_ref[...]
    if adaln:
        y = y * (1.0 + mod_ref[1:2, :]) + mod_ref[0:1, :]
    h_ref[...] = y.astype(h_ref.dtype)


def _norm_call(x, gain, mod=None, res=None, *, t_batch, name):
    n, d = x.shape
    tm = min(512, t_batch)
    per = t_batch // tm
    adaln = mod is not None
    has_res = res is not None
    row = pl.BlockSpec((tm, d), lambda i: (i, 0))
    modspec = pl.BlockSpec((None, 8, d), lambda i: (i // per, 0, 0))
    args, specs = [x], [row]
    if has_res:
        y_all, row0, mod_prev = res
        assert row0 % tm == 0
        args += [y_all, mod_prev]
        specs += [pl.BlockSpec((tm, d), lambda i: (i + row0 // tm, 0)), modspec]
    args.append(gain.reshape(1, d))
    specs.append(pl.BlockSpec((1, d), lambda i: (0, 0)))
    if adaln:
        args.append(mod)
        specs.append(modspec)
    out_shape, out_specs = [], []
    if has_res and adaln:
        out_shape.append(jax.ShapeDtypeStruct((n, d), F32))
        out_specs.append(row)
    out_shape.append(jax.ShapeDtypeStruct((n, d), BF16 if adaln else F32))
    out_specs.append(row)
    outs = pl.pallas_call(
        functools.partial(_norm_body, has_res=has_res, adaln=adaln),
        grid=(n // tm,),
        in_specs=specs,
        out_specs=out_specs,
        out_shape=out_shape,
        compiler_params=_params(1, VMEM_LIMIT),
        name=name,
    )(*args)
    if has_res and adaln:
        return outs[0], outs[1]
    return (x, outs[0]) if adaln else outs[0]


def _norm2_body(*refs, has_res, steps_p):
    it = iter(refs)
    x_p, x_s = next(it), next(it)
    if has_res:
        y_ref, modp_p, modp_s = next(it), next(it), next(it)
    g_ref, mod_p, mod_s = next(it), next(it), next(it)
    if has_res:
        xo_p, xo_s = next(it), next(it)
    h_ref = next(it)

    def run(x_ref, mod_ref, modp_ref, xo_ref):
        x = x_ref[...]
        if has_res:
            x = x + modp_ref[5:6, :] * y_ref[...]
            xo_ref[...] = x
        h_ref[...] = ((_rms(x) * g_ref[...]) * (1.0 + mod_ref[1:2, :]) + mod_ref[0:1, :]).astype(h_ref.dtype)

    i = pl.program_id(0)

    @pl.when(i < steps_p)
    def _():
        run(x_p, mod_p, modp_p if has_res else None, xo_p if has_res else None)

    @pl.when(i >= steps_p)
    def _():
        run(x_s, mod_s, modp_s if has_res else None, xo_s if has_res else None)


def _norm2_call(x_p, x_s, gain, mod_p, mod_s, res, *, t_s, name):
    n_p, d = x_p.shape
    n_s = x_s.shape[0]
    tm = min(512, n_p, t_s)
    assert n_p % tm == 0 and t_s % tm == 0
    sp, ss = n_p // tm, n_s // tm
    per_s = t_s // tm
    has_res = res is not None
    row_p = pl.BlockSpec((tm, d), lambda i: (jnp.minimum(i, sp - 1), 0))
    row_s = pl.BlockSpec((tm, d), lambda i: (jnp.maximum(i - sp, 0), 0))
    row_all = pl.BlockSpec((tm, d), lambda i: (i, 0))
    m_p = pl.BlockSpec((None, 8, d), lambda i: (0, 0, 0))
    m_s = pl.BlockSpec((None, 8, d), lambda i: (jnp.maximum(i - sp, 0) // per_s, 0, 0))
    args, specs = [x_p, x_s], [row_p, row_s]
    if has_res:
        args += list(res)
        specs += [row_all, m_p, m_s]
    args += [gain.reshape(1, d), mod_p, mod_s]
    specs += [pl.BlockSpec((1, d), lambda i: (0, 0)), m_p, m_s]
    out_shape, out_specs = [], []
    if has_res:
        out_shape += [jax.ShapeDtypeStruct((n_p, d), F32), jax.ShapeDtypeStruct((n_s, d), F32)]
        out_specs += [row_p, row_s]
    out_shape.append(jax.ShapeDtypeStruct((n_p + n_s, d), BF16))
    out_specs.append(row_all)
    outs = pl.pallas_call(
        functools.partial(_norm2_body, has_res=has_res, steps_p=sp),
        grid=(sp + ss,),
        in_specs=specs,
        out_specs=out_specs,
        out_shape=out_shape,
        compiler_params=_params(1, VMEM_LIMIT),
        name=name,
    )(*args)
    return (outs[0], outs[1], outs[2]) if has_res else (x_p, x_s, outs[0])


def _rope_swap(blk):
    lane = lax.broadcasted_iota(jnp.int32, blk.shape, 1)
    return jnp.where((lane & 32) == 0, pltpu.roll(blk, LANES - 32, 1), pltpu.roll(blk, 32, 1))


def _inproj_body(h_ref, w_ref, cs_ref, sn_ref, o_ref, wbf_ref, *, tiles_p, tn):
    n = pl.program_id(0)
    m = pl.program_id(1)

    @pl.when(m == 0)
    def _():
        wbf_ref[...] = w_ref[...].astype(BF16)

    groups = tn // LANES
    latent = m >= tiles_p
    qk_attn = n < (OFF_VA // tn)
    qk_ret = n == (OFF_QR // tn)

    def tile(rotate, scale_k):
        acc = jnp.dot(h_ref[...], wbf_ref[...], preferred_element_type=F32)
        if not rotate and not scale_k:
            o_ref[...] = acc
            return
        for j in range(groups):
            blk = acc[:, j * LANES:(j + 1) * LANES]
            if scale_k and j * LANES >= W_QR:
                blk = blk * (D_K_R ** -0.5)
            if rotate:
                blk = blk * cs_ref[...] + _rope_swap(blk) * sn_ref[...]
            o_ref[:, j * LANES:(j + 1) * LANES] = blk

    @pl.when(qk_attn & latent)
    def _():
        tile(True, False)

    @pl.when(qk_ret & latent)
    def _():
        tile(True, True)

    @pl.when(qk_ret & jnp.logical_not(latent))
    def _():
        tile(False, True)

    @pl.when(jnp.logical_not(qk_ret | (qk_attn & latent)))
    def _():
        tile(False, False)


def _inproj_call(h_all, w_in, layer, rope_tabs, *, n_p, t_s, name):
    n_tok, d = h_all.shape
    d_in = w_in.shape[-1]
    tn = 1024
    assert OFF_QR % tn == 0 and OFF_QR + 2 * W_QR == OFF_QR + tn and d_in % tn == 0
    tm = min(1024, n_p, t_s)
    assert n_p % tm == 0 and t_s % tm == 0
    tiles_p, per = n_p // tm, t_s // tm
    tab = pl.BlockSpec((tm, LANES), lambda n, m: (jnp.maximum(m - tiles_p, 0) % per, 0))
    return pl.pallas_call(
        functools.partial(_inproj_body, tiles_p=tiles_p, tn=tn),
        grid=(d_in // tn, n_tok // tm),
        in_specs=[
            pl.BlockSpec((tm, d), lambda n, m: (m, 0)),
            pl.BlockSpec((None, d, tn), lambda n, m: (layer, 0, n)),
            tab, tab,
        ],
        out_specs=pl.BlockSpec((tm, tn), lambda n, m: (m, n)),
        out_shape=jax.ShapeDtypeStruct((n_tok, d_in), F32),
        scratch_shapes=[pltpu.VMEM((d, tn), BF16)],
        compiler_params=_params(2, VMEM_LIMIT),
        name=name,
    )(h_all, w_in, *rope_tabs)


SHIFT_SAFE = 40.0


def _attn_body(*refs, hb, t_new, past, lam_init, emit_kv, n_prev, bounded):
    it = iter(refs)
    lp_ref, q_ref, k_ref, v_ref = next(it), next(it), next(it), next(it)
    if past:
        ck_ref, cv_ref = next(it), next(it)
    g_ref = next(it)
    if emit_kv and n_prev:
        kp_ref, vp_ref = next(it), next(it)
    o_ref = next(it)
    if emit_kv:
        ko_ref, vo_ref = next(it), next(it)
    kbf, vbf = next(it), next(it)
    if bounded:
        kmax = next(it)
    i = pl.program_id(2)

    def map_mask(shape, mp):
        lane = lax.broadcasted_iota(jnp.int32, shape, 1)
        return (lane % LANES >= mp * D_HEAD_A) & (lane % LANES < (mp + 1) * D_HEAD_A)

    @pl.when(i == 0)
    def _():
        if past:
            kbf[0:past, :] = ck_ref[...].astype(BF16)
            vbf[0:past, :] = cv_ref[...].astype(BF16)
        kbf[past:past + t_new, :] = k_ref[...].astype(BF16)
        vbf[past:past + t_new, :] = v_ref[...].astype(BF16)
        if emit_kv:
            if n_prev:
                ko_ref[0:n_prev] = kp_ref[...]
                vo_ref[0:n_prev] = vp_ref[...]
            ko_ref[n_prev] = k_ref[...]
            vo_ref[n_prev] = v_ref[...]
        if bounded:
            k2 = kbf[...].astype(F32)
            k2 = k2 * k2
            for mp in range(2):
                n2 = jnp.sum(jnp.where(map_mask(k2.shape, mp), k2, 0.0), axis=-1, keepdims=True)
                kmax[mp] = jnp.broadcast_to(jnp.sqrt(jnp.max(n2, axis=0, keepdims=True)), kmax.shape[1:])

    lp = lp_ref[...]
    lam = (jnp.exp(jnp.sum(lp[0:1, :] * lp[1:2, :], keepdims=True))
           - jnp.exp(jnp.sum(lp[2:3, :] * lp[3:4, :], keepdims=True)) + lam_init)
    nt = (((1,), (1,)), ((), ()))

    def head(j, shifts):
        sl = slice(j * LANES, (j + 1) * LANES)
        k = kbf[:, sl]
        ps, rs = [], []
        for mp in range(2):
            s = lax.dot_general(qms[j][mp], k, nt, preferred_element_type=F32)
            c = jnp.max(s, axis=-1, keepdims=True) if shifts is None else shifts[mp]
            p = jnp.exp(s - c)
            ps.append(p)
            rs.append(jnp.sum(p, axis=-1, keepdims=True))
        a = (ps[0] * (1.0 / rs[0]) - ps[1] * (lam / rs[1])).astype(BF16)
        o = jnp.dot(a, vbf[:, sl], preferred_element_type=F32)
        o_ref[:, sl] = ((_rms(o) * g_ref[...]) * (1.0 - lam_init)).astype(o_ref.dtype)

    qms = []
    for j in range(hb):
        q = q_ref[:, j * LANES:(j + 1) * LANES] * (D_HEAD_A ** -0.5)
        qms.append([jnp.where(map_mask(q.shape, mp), q, 0.0).astype(BF16) for mp in range(2)])

    if not bounded:
        for j in range(hb):
            head(j, None)
        return

    assert hb == 1
    shifts = []
    for mp in range(2):
        qf = qms[0][mp].astype(F32)
        qn = jnp.sqrt(jnp.sum(qf * qf, axis=-1, keepdims=True))
        shifts.append(qn * kmax[mp][0:1, 0:1] * 1.001 + 1e-6)
    safe = jnp.max(jnp.maximum(shifts[0], shifts[1])) <= SHIFT_SAFE

    @pl.when(safe)
    def _():
        head(0, shifts)

    @pl.when(jnp.logical_not(safe))
    def _():
        head(0, None)


def _attn_call(proj, lambda_p, diff_norm, cache, layer, *, row0, n_batch, t_batch, lam_init, hb, emit_kv, kv_prev,
               name):
    n_tok = n_batch * t_batch
    tq = min(256, t_batch)
    qsteps = t_batch // tq
    assert row0 % t_batch == 0
    q0, b0 = row0 // tq, row0 // t_batch
    w = hb * LANES
    past = 0 if cache is None else cache[0].shape[2]
    assert not emit_kv or (w == W_QA and qsteps == 1)
    args = [lambda_p, proj, proj, proj]
    specs = [
        pl.BlockSpec((None, 4, D_HEAD_A), lambda b, h, i: (layer, 0, 0)),
        pl.BlockSpec((tq, w), lambda b, h, i: (q0 + b * qsteps + i, OFF_QA // w + h)),
        pl.BlockSpec((t_batch, w), lambda b, h, i: (b0 + b, OFF_KA // w + h)),
        pl.BlockSpec((t_batch, w), lambda b, h, i: (b0 + b, OFF_VA // w + h)),
    ]
    if past:
        cspec = pl.BlockSpec((None, None, past, w), lambda b, h, i: (b, layer, 0, h))
        args += [cache[0], cache[1]]
        specs += [cspec, cspec]
    args.append(diff_norm)
    specs.append(pl.BlockSpec((None, 1, D_V_A), lambda b, h, i: (layer, 0, 0)))
    t_keys = past + t_batch
    out_specs = [pl.BlockSpec((tq, w), lambda b, h, i: (b * qsteps + i, h))]
    out_shape = [jax.ShapeDtypeStruct((n_tok, W_VA), BF16)]
    n_prev = 0
    if emit_kv:
        n_prev = 0 if kv_prev is None else kv_prev[0].shape[1]
        if n_prev:
            args += list(kv_prev)
            specs += [pl.BlockSpec((None, n_prev, t_batch, w), lambda b, h, i: (b, 0, 0, 0))] * 2
        out_specs += [pl.BlockSpec((None, n_prev + 1, t_batch, w), lambda b, h, i: (b, 0, 0, 0))] * 2
        out_shape += [jax.ShapeDtypeStruct((n_batch, n_prev + 1, t_batch, w), F32)] * 2
    bounded = hb == 1
    scratch = [pltpu.VMEM((t_keys, w), BF16), pltpu.VMEM((t_keys, w), BF16)]
    if bounded:
        scratch.append(pltpu.VMEM((2, 8, LANES), F32))
    outs = pl.pallas_call(
        functools.partial(_attn_body, hb=hb, t_new=t_batch, past=past, lam_init=lam_init, emit_kv=emit_kv,
                          n_prev=n_prev, bounded=bounded),
        grid=(n_batch, N_HEADS_A // hb, qsteps),
        in_specs=specs,
        out_specs=out_specs,
        out_shape=out_shape,
        scratch_shapes=scratch,
        compiler_params=_params(3, VMEM_LIMIT),
        name=name,
    )(*args)
    return outs if emit_kv else outs[0]


def _ret_body(*refs, hb, nc, c_, has_state):
    if has_state:
        lg_ref, q_ref, k_ref, v_ref, s0_ref, of_ref, ob_ref, sf_ref, st_ref, dec_ref, vec_ref, cd_ref = refs
    else:
        lg_ref, q_ref, k_ref, v_ref, of_ref, ob_ref, sf_ref, st_ref, dec_ref, vec_ref, cd_ref = refs
    hblk = pl.program_id(1)
    half = D_K_R

    rel = (lax.broadcasted_iota(jnp.int32, (c_, c_), 0) - lax.broadcasted_iota(jnp.int32, (c_, c_), 1)).astype(F32)
    pos = lax.broadcasted_iota(jnp.int32, (c_, LANES), 0).astype(F32)
    for j in range(hb):
        for d in range(2):
            lg = lg_ref[d, hblk * hb + j]
            if d == 0:
                dec_ref[d, j] = jnp.where(rel >= 0, jnp.exp(jnp.maximum(rel, 0.0) * lg), 0.0)
                vec_ref[d, j, 0] = jnp.exp((pos + 1.0) * lg)
                vec_ref[d, j, 1] = jnp.exp((c_ - 1.0 - pos) * lg)
            else:
                dec_ref[d, j] = jnp.where(rel <= 0, jnp.exp(jnp.maximum(-rel, 0.0) * lg), 0.0)
                vec_ref[d, j, 0] = jnp.exp((c_ - pos) * lg)
                vec_ref[d, j, 1] = jnp.exp(pos * lg)
            cd_ref[d, j] = jnp.exp(jnp.full((LANES, D_V_R), float(c_), F32) * lg)
            st_ref[d, j] = jnp.zeros((LANES, D_V_R), F32)
            if has_state:
                lo = (j % 2) * half
                st_ref[d, j, lo:lo + half, :] = s0_ref[d, j]

    nt = (((1,), (1,)), ((), ()))

    def step(i, carry):
        for d in range(2):
            c = i if d == 0 else nc - 1 - i
            r0 = pl.multiple_of(c * c_, c_)
            for j in range(hb):
                jp = j // 2
                q = q_ref[pl.ds(r0, c_), jp * LANES:(jp + 1) * LANES]
                k = k_ref[pl.ds(r0, c_), jp * LANES:(jp + 1) * LANES]
                v = v_ref[pl.ds(r0, c_), j * D_V_R:(j + 1) * D_V_R].astype(BF16)
                lane = lax.broadcasted_iota(jnp.int32, q.shape, 1)
                mine = (lane >= (j % 2) * half) & (lane < (j % 2 + 1) * half)
                qm = jnp.where(mine, q, 0.0).astype(BF16)
                s = lax.dot_general(qm, k.astype(BF16), nt, preferred_element_type=F32) * dec_ref[d, j]
                slab = st_ref[d, j]
                o = (jnp.dot(s.astype(BF16), v, preferred_element_type=F32)
                     + jnp.dot(qm, slab.astype(BF16), preferred_element_type=F32) * vec_ref[d, j, 0])
                kzt = (k * vec_ref[d, j, 1]).T.astype(BF16)
                st_ref[d, j] = slab * cd_ref[d, j] + jnp.dot(kzt, v, preferred_element_type=F32)
                if d == 0:
                    of_ref[pl.ds(r0, c_), j * D_V_R:(j + 1) * D_V_R] = o
                else:
                    ob_ref[pl.ds(r0, c_), j * D_V_R:(j + 1) * D_V_R] = o
        return carry

    lax.fori_loop(0, nc, step, 0, unroll=4 if nc % 4 == 0 else 1)
    for j in range(hb):
        lo = (j % 2) * half
        for d in range(2):
            sf_ref[d, j] = st_ref[d, j, lo:lo + half, :]


def _ret_call(proj, log_g, state, layer, *, row0, n_batch, t_batch, hb, name):
    n_tok = n_batch * t_batch
    assert row0 % t_batch == 0
    b0 = row0 // t_batch
    c_ = min(RET_TILE, t_batch)
    nc = t_batch // c_
    wq, wv = hb * D_K_R, hb * D_V_R
    args = [log_g, proj, proj, proj]
    specs = [
        pl.BlockSpec(memory_space=pltpu.SMEM),
        pl.BlockSpec((t_batch, wq), lambda b, h: (b0 + b, OFF_QR // wq + h)),
        pl.BlockSpec((t_batch, wq), lambda b, h: (b0 + b, OFF_KR // wq + h)),
        pl.BlockSpec((t_batch, wv), lambda b, h: (b0 + b, OFF_VR // wv + h)),
    ]
    has_state = state is not None
    if has_state:
        args.append(state)
        specs.append(pl.BlockSpec((None, None, 2, hb, D_K_R, D_V_R), lambda b, h: (b, layer, 0, h, 0, 0)))
    ospec = pl.BlockSpec((t_batch, wv), lambda b, h: (b, h))
    return pl.pallas_call(
        functools.partial(_ret_body, hb=hb, nc=nc, c_=c_, has_state=has_state),
        grid=(n_batch, N_HEADS_R // hb),
        in_specs=specs,
        out_specs=[ospec, ospec, pl.BlockSpec((None, 2, hb, D_K_R, D_V_R), lambda b, h: (b, 0, h, 0, 0))],
        out_shape=[
            jax.ShapeDtypeStruct((n_tok, W_VR), F32),
            jax.ShapeDtypeStruct((n_tok, W_VR), F32),
            jax.ShapeDtypeStruct((n_batch, 2, N_HEADS_R, D_K_R, D_V_R), F32),
        ],
        scratch_shapes=[
            pltpu.VMEM((2, hb, LANES, D_V_R), F32),
            pltpu.VMEM((2, hb, c_, c_), F32),
            pltpu.VMEM((2, hb, 2, c_, LANES), F32),
            pltpu.VMEM((2, hb, LANES, D_V_R), F32),
        ],
        compiler_params=_params(2, VMEM_LIMIT),
        name=name,
    )(*args)


def _mix_body(oa_ref, of_ref, ob_ref, u_ref, v_ref, gr_ref, ga_ref, gb_ref, gc_ref, sgn_ref, sgw_ref, sgb_ref,
              rn_ref, wa_ref, wb_ref, wc_ref, o_ref, sb_s, oc_s, *, tm):
    for j in range(N_HEADS_R):
        sl = slice(j * D_V_R, (j + 1) * D_V_R)
        y = _rms(of_ref[:, sl] + ob_ref[:, sl]) * rn_ref[:, sl]
        oc_s[:, sl] = (_silu(gr_ref[:, sl]) * y).astype(BF16)
    gv = _gelu_tanh(v_ref[...])
    vn = (_rms(gv) * sgn_ref[...]).astype(BF16)
    for g in range(N_GROUPS_B):
        sl = slice(g * D_GROUP_B, (g + 1) * D_GROUP_B)
        wg = sgw_ref[g].astype(BF16)
        bias = sgb_ref[:, g:g + 1]
        for c in range(tm // CHUNK):
            rows = slice(c * CHUNK, (c + 1) * CHUNK)
            mixed = jnp.dot(wg, vn[rows, sl], preferred_element_type=F32) + bias
            sb_s[rows, sl] = (_gelu_tanh(u_ref[rows, sl]) * mixed).astype(BF16)

    ya = jnp.dot(oa_ref[...], wa_ref[...], preferred_element_type=F32)
    yb = jnp.dot(sb_s[...], wb_ref[...], preferred_element_type=F32)
    yc = jnp.dot(oc_s[...], wc_ref[...], preferred_element_type=F32)
    merged = _sigmoid(ga_ref[...]) * ya + _sigmoid(gb_ref[...]) * yb + _sigmoid(gc_ref[...]) * yc
    o_ref[...] = merged.astype(o_ref.dtype)


def _mix_call(oa, o_f, o_b, proj, sg_norm, sg_w, sg_bt, ret_norm, wa, wb, wc, layer, *, row0, t_batch, name):
    n_tok = oa.shape[0]
    d = wa.shape[-1]
    tm = min(256, t_batch)
    assert row0 % tm == 0
    m0 = row0 // tm
    wide = pl.BlockSpec((tm, W_B), lambda m: (m, 0))

    def pcol(off, width):
        return pl.BlockSpec((tm, width), lambda m: (m0 + m, off // width))

    def lvec(width):
        return pl.BlockSpec((None, 1, width), lambda m: (layer, 0, 0))

    wspec = pl.BlockSpec((None, W_B, d), lambda m: (layer, 0, 0), pipeline_mode=pl.Buffered(1))
    return pl.pallas_call(
        functools.partial(_mix_body, tm=tm),
        grid=(n_tok // tm,),
        in_specs=[
            wide, wide, wide, pcol(OFF_U, W_B), pcol(OFF_V, W_B), pcol(OFF_GR, W_VR),
            pcol(OFF_GA, d), pcol(OFF_GA + d, d), pcol(OFF_GA + 2 * d, d),
            lvec(W_B),
            pl.BlockSpec((None, N_GROUPS_B, CHUNK, CHUNK), lambda m: (layer, 0, 0, 0)),
            pl.BlockSpec((None, CHUNK, N_GROUPS_B), lambda m: (layer, 0, 0)),
            lvec(W_VR), wspec, wspec, wspec,
        ],
        out_specs=pl.BlockSpec((tm, d), lambda m: (m, 0)),
        out_shape=jax.ShapeDtypeStruct((n_tok, d), BF16),
        scratch_shapes=[pltpu.VMEM((tm, W_B), BF16), pltpu.VMEM((tm, W_VR), BF16)],
        compiler_params=_params(1, VMEM_LIMIT),
        name=name,
    )(oa, o_f, o_b, proj, proj, proj, proj, proj, proj, sg_norm, sg_w, sg_bt, ret_norm, wa, wb, wc)


def _route(logits):
    lane = lax.broadcasted_iota(jnp.int32, logits.shape, 1).astype(F32)
    big = float(1 << 20)
    neg = -jnp.inf
    lgm = jnp.where(lane < N_EXPERT_GROUPS, logits, neg)
    mx = jnp.max(lgm, axis=-1, keepdims=True)
    p_group = 1.0 / jnp.sum(jnp.exp(lgm - mx), axis=-1, keepdims=True)
    gsel = jnp.min(jnp.where(lgm == mx, lane, big), axis=-1, keepdims=True)
    lo = N_EXPERT_GROUPS + gsel * EXPERTS_PER_GROUP
    insel = (lane >= lo) & (lane < lo + EXPERTS_PER_GROUP)
    le = jnp.where(insel, logits, neg)
    v1 = jnp.max(le, axis=-1, keepdims=True)
    i1 = jnp.min(jnp.where(le == v1, lane, big), axis=-1, keepdims=True)
    le2 = jnp.where(lane == i1, neg, le)
    v2 = jnp.max(le2, axis=-1, keepdims=True)
    i2 = jnp.min(jnp.where(le2 == v2, lane, big), axis=-1, keepdims=True)
    e2 = jnp.exp(v2 - v1)
    den = 1.0 + e2
    w1 = p_group * (1.0 / den)
    w2 = p_group * (e2 / den)
    return jnp.where(lane == i1, w1, 0.0) + jnp.where(lane == i2, w2, 0.0), gsel


def _outproj_body(mg_p, x_p, mod_p, mg_s, x_s, mod_s, w_ref, g_ref, wr_ref, br_ref, x1_p, x1_s, h2_ref, rt_ref, *,
                  steps_p):
    def run(mg_ref, x_ref, mod_ref, x1_ref):
        y = jnp.dot(mg_ref[...], w_ref[...], preferred_element_type=F32)
        x1 = x_ref[...] + mod_ref[2:3, :] * y
        x1_ref[...] = x1
        h2 = (_rms(x1) * g_ref[...]) * (1.0 + mod_ref[4:5, :]) + mod_ref[3:4, :]
        h2_ref[...] = h2
        logits = jnp.dot(h2.astype(BF16), wr_ref[...], preferred_element_type=F32) + br_ref[...]
        _, gsel = _route(logits)
        rt_ref[...] = jnp.broadcast_to(gsel, rt_ref.shape)

    i = pl.program_id(0)

    @pl.when(i < steps_p)
    def _():
        run(mg_p, x_p, mod_p, x1_p)

    @pl.when(i >= steps_p)
    def _():
        run(mg_s, x_s, mod_s, x1_s)


def _outproj_call(merged_p, x_p, mod_p, merged_s, x_s, mod_s, w_out, norm2, w_router, b_router, layer, *, t_s, name):
    n_p, d = x_p.shape
    n_s = x_s.shape[0]
    tm = 256
    assert n_p % tm == 0 and t_s % tm == 0
    sp, ss = n_p // tm, n_s // tm
    per_s = t_s // tm
    row_p = pl.BlockSpec((tm, d), lambda i: (jnp.minimum(i, sp - 1), 0))
    row_s = pl.BlockSpec((tm, d), lambda i: (jnp.maximum(i - sp, 0), 0))
    return pl.pallas_call(
        functools.partial(_outproj_body, steps_p=sp),
        grid=(sp + ss,),
        in_specs=[
            row_p, row_p, pl.BlockSpec((None, 8, d), lambda i: (0, 0, 0)),
            row_s, row_s, pl.BlockSpec((None, 8, d), lambda i: (jnp.maximum(i - sp, 0) // per_s, 0, 0)),
            pl.BlockSpec((None, d, d), lambda i: (layer, 0, 0)),
            pl.BlockSpec((None, 1, d), lambda i: (layer, 0, 0)),
            pl.BlockSpec((None, d, ROUTER_LANES), lambda i: (layer, 0, 0)),
            pl.BlockSpec((None, 1, ROUTER_LANES), lambda i: (layer, 0, 0)),
        ],
        out_specs=[row_p, row_s, pl.BlockSpec((tm, d), lambda i: (i, 0)),
                   pl.BlockSpec((tm, ROUTER_LANES), lambda i: (i, 0))],
        out_shape=[
            jax.ShapeDtypeStruct((n_p, d), F32),
            jax.ShapeDtypeStruct((n_s, d), F32),
            jax.ShapeDtypeStruct((n_p + n_s, d), F32),
            jax.ShapeDtypeStruct((n_p + n_s, ROUTER_LANES), F32),
        ],
        compiler_params=_params(1, VMEM_LIMIT),
        name=name,
    )(merged_p, x_p, mod_p, merged_s, x_s, mod_s, w_out, norm2, w_router, b_router)


MOE_FIRST, MOE_LAST, MOE_VALID = 1, 2, 4


def _moe_plan(gsel, tm):
    n = gsel.shape[0]
    nt = n // tm
    n_items = nt + N_EXPERT_GROUPS - 1
    perm = jnp.argsort(gsel, stable=True).astype(jnp.int32)
    gs = gsel[perm]
    gf, gl = gs[0::tm], gs[tm - 1::tm]
    grp = jnp.arange(N_EXPERT_GROUPS, dtype=jnp.int32)[None, :]
    active = ((grp >= gf[:, None]) & (grp <= gl[:, None])).reshape(-1)
    order = jnp.argsort(jnp.logical_not(active), stable=True).astype(jnp.int32)[:n_items]
    valid = active[order]
    last_real = order[jnp.sum(active.astype(jnp.int32)) - 1]
    item = jnp.where(valid, order, last_real)
    wt, wg = item // N_EXPERT_GROUPS, item % N_EXPERT_GROUPS
    flags = (jnp.where(valid, MOE_VALID, 0) + jnp.where(valid & (wg == gf[wt]), MOE_FIRST, 0)
             + jnp.where(valid & (wg == gl[wt]), MOE_LAST, 0)).astype(jnp.int32)
    return perm, wt.astype(jnp.int32), wg.astype(jnp.int32), flags


def _moe_body(perm_ref, wt_ref, wg_ref, fl_ref, h_hbm, wr_ref, br_ref, wgate_ref, wup_ref, wdown_ref, y_hbm,
              h32, hbf, acc, dws, sem, *, tm):
    w = pl.program_id(0)
    e = pl.program_id(1)
    flags = fl_ref[w]
    valid = (flags & MOE_VALID) != 0
    base = wt_ref[w] * tm

    def row_in(r):
        return pltpu.make_async_copy(h_hbm.at[pl.ds(perm_ref[base + r], 1), :], h32.at[pl.ds(r, 1), :], sem.at[0])

    def row_out(r):
        return pltpu.make_async_copy(acc.at[pl.ds(r, 1), :], y_hbm.at[pl.ds(perm_ref[base + r], 1), :], sem.at[1])

    def for_rows(fn):
        def body(r8, c):
            for s in range(8):
                fn(pl.multiple_of(r8 * 8, 8) + s)
            return c
        lax.fori_loop(0, tm // 8, body, 0)

    def start_rows(copy):
        for r in range(tm):
            copy(r).start()

    @pl.when(valid & ((flags & MOE_FIRST) != 0) & (e == 0))
    def _():
        start_rows(row_in)
        for_rows(lambda r: row_in(r).wait())
        hb = h32[...].astype(BF16)
        hbf[...] = hb
        logits = jnp.dot(hb, wr_ref[...], preferred_element_type=F32) + br_ref[...]
        dws[...], _ = _route(logits)
        acc[...] = jnp.zeros(acc.shape, F32)

    @pl.when(valid)
    def _():
        h = hbf[...]
        a = jnp.dot(h, wgate_ref[...].astype(BF16), preferred_element_type=F32)
        up = jnp.dot(h, wup_ref[...].astype(BF16), preferred_element_type=F32)
        lane = lax.broadcasted_iota(jnp.int32, dws.shape, 1)
        ex = N_EXPERT_GROUPS + wg_ref[w] * EXPERTS_PER_GROUP + e
        wgt = jnp.sum(jnp.where(lane == ex, dws[...], 0.0), axis=-1, keepdims=True)
        act = ((_silu(a) * up) * wgt).astype(BF16)
        acc[...] += jnp.dot(act, wdown_ref[...].astype(BF16), preferred_element_type=F32)

    @pl.when(valid & ((flags & MOE_LAST) != 0) & (e == EXPERTS_PER_GROUP - 1))
    def _():
        start_rows(row_out)
        for_rows(lambda r: row_out(r).wait())


def _moe_call(h2_all, grp_all, w_router, b_router, w_gate, w_up, w_down, layer, *, name):
    n, d = h2_all.shape
    f = w_gate.shape[-1]
    tm = 1024 if n % 1024 == 0 and n >= 2048 else 256
    perm, wt, wg, flags = _moe_plan(grp_all[:, 0].astype(jnp.int32), tm)
    n_items = wt.shape[0]

    def expert(w, e, perm_ref, wt_ref, wg_ref, fl_ref):
        e_eff = jnp.where((fl_ref[w] & MOE_VALID) != 0, e, EXPERTS_PER_GROUP - 1)
        return (layer, wg_ref[w] * EXPERTS_PER_GROUP + e_eff, 0, 0)

    return pl.pallas_call(
        functools.partial(_moe_body, tm=tm),
        grid_spec=pltpu.PrefetchScalarGridSpec(
            num_scalar_prefetch=4,
            grid=(n_items, EXPERTS_PER_GROUP),
            in_specs=[
                pl.BlockSpec(memory_space=pl.ANY),
                pl.BlockSpec((None, d, ROUTER_LANES), lambda w, e, *_: (layer, 0, 0)),
                pl.BlockSpec((None, 1, ROUTER_LANES), lambda w, e, *_: (layer, 0, 0)),
                pl.BlockSpec((None, None, d, f), expert),
                pl.BlockSpec((None, None, d, f), expert),
                pl.BlockSpec((None, None, f, d), expert),
            ],
            out_specs=pl.BlockSpec(memory_space=pl.ANY),
            scratch_shapes=[
                pltpu.VMEM((tm, d), F32),
                pltpu.VMEM((tm, d), BF16),
                pltpu.VMEM((tm, d), F32),
                pltpu.VMEM((tm, ROUTER_LANES), F32),
                pltpu.SemaphoreType.DMA((2,)),
            ],
        ),
        out_shape=jax.ShapeDtypeStruct((n, d), F32),
        compiler_params=_params(2, VMEM_LIMIT),
        name=name,
    )(perm, wt, wg, flags, h2_all, w_router, b_router, w_gate, w_up, w_down)


def _rope_tables(n_tok):
    rows = n_tok // GRID_W
    row = jnp.repeat(jnp.arange(rows, dtype=F32), GRID_W)
    col = jnp.tile(jnp.arange(GRID_W, dtype=F32), rows)
    n_freq = ROPE_DIM // 4
    inv = ROPE_BASE ** (-jnp.arange(n_freq, dtype=F32) / n_freq)
    ang = jnp.concatenate([row[:, None] * inv, col[:, None] * inv], axis=-1)
    cos, sin = jnp.cos(ang), jnp.sin(ang)
    return jnp.concatenate([cos, cos, cos, cos], axis=-1), jnp.concatenate([-sin, sin, -sin, sin], axis=-1)


def kernel(x_prompt, x_sample, cache_k, cache_v, state_ret, c, c_ctx, w_mod, b_mod, norm1, w_in, lambda_p, diff_norm, sg_norm, sg_w, sg_b, ret_decay, ret_norm, w_up_a, w_up_b, w_up_c, w_out, norm2, w_rg, b_rg, w_re, b_re, w_e_gate, w_e_up, w_e_down, final_norm):
    depth = w_in.shape[0]
    bp, tp, d = x_prompt.shape
    bs, ts, _ = x_sample.shape
    past = cache_k.shape[2]

    cond8 = jnp.zeros((8, d), F32).at[0].set(c_ctx).at[1:1 + bs].set(c)
    mods = _mod_call(cond8, w_mod, b_mod).reshape(depth, 8, N_MOD, d)
    mods = jnp.pad(mods, ((0, 0), (0, 0), (0, 8 - N_MOD), (0, 0)))

    wa_b, wb_b, wc_b, wo_b = (w.astype(BF16) for w in (w_up_a, w_up_b, w_up_c, w_out))
    w_router = jnp.concatenate(
        [w_rg, w_re, jnp.zeros((depth, d, ROUTER_LANES - N_EXPERT_GROUPS - N_EXPERTS), F32)], axis=-1).astype(BF16)
    b_router = jnp.concatenate(
        [b_rg, b_re, jnp.zeros((depth, ROUTER_LANES - N_EXPERT_GROUPS - N_EXPERTS), F32)], axis=-1)[:, None, :]
    log_g = jax.nn.log_sigmoid(ret_decay.astype(F32))
    sg_bt = jnp.swapaxes(sg_b, 1, 2)
    rope_tabs = _rope_tables(ts)
    cache_k2 = cache_k.reshape(bs, depth, past, W_QA)
    cache_v2 = cache_v.reshape(bs, depth, past, W_VA)

    groups = [
        dict(tag="p", x=x_prompt.reshape(bp * tp, d), nb=bp, t=tp, tt=bp * tp, rope=None, ctx=False, hb_a=8, hb_r=8),
        dict(tag="s", x=x_sample.reshape(bs * ts, d), nb=bs, t=ts, tt=ts, rope=rope_tabs, ctx=True, hb_a=1, hb_r=2),
    ]
    kv_new, ss = None, []
    row0 = 0
    for g in groups:
        g["res"] = None
        g["row0"] = row0
        row0 += g["nb"] * g["t"]
    gp, gs_ = groups
    y_prev = None
    for l in range(depth):
        lam_init = 0.8 - 0.6 * math.exp(-0.3 * l)
        mod_p, mod_s = mods[l, 0:1], mods[l, 1:1 + bs]
        res = None if y_prev is None else (y_prev, gp["mod"], gs_["mod"])
        gp["x"], gs_["x"], h_all = _norm2_call(gp["x"], gs_["x"], norm1[l], mod_p, mod_s, res, t_s=ts,
                                               name=f"norm1_{l}")
        gp["mod"], gs_["mod"] = mod_p, mod_s
        proj = _inproj_call(h_all, w_in, l, rope_tabs, n_p=bp * tp, t_s=ts, name=f"inproj{l}")
        for g in groups:
            tag, nb, t, tt = g["tag"], g["nb"], g["t"], g["tt"]
            cache = (cache_k2, cache_v2) if g["ctx"] else None
            oa = _attn_call(proj, lambda_p, diff_norm[:, None, :], cache, l, row0=g["row0"], n_batch=nb, t_batch=t,
                            lam_init=lam_init, hb=g["hb_a"], emit_kv=not g["ctx"], kv_prev=kv_new,
                            name=f"attn_{tag}{l}")
            if not g["ctx"]:
                oa, kv_new = oa[0], (oa[1], oa[2])
            o_f, o_b, s_fin = _ret_call(proj, log_g[l], state_ret if g["ctx"] else None, l, row0=g["row0"],
                                        n_batch=nb, t_batch=t, hb=g["hb_r"], name=f"ret_{tag}{l}")
            g["merged"] = _mix_call(oa, o_f, o_b, proj, sg_norm[:, None, :], sg_w, sg_bt, ret_norm[:, None, :],
                                    wa_b, wb_b, wc_b, l, row0=g["row0"], t_batch=tt, name=f"mix_{tag}{l}")
            if not g["ctx"]:
                ss.append(s_fin)
        gp["x"], gs_["x"], h2_all, grp_all = _outproj_call(
            gp["merged"], gp["x"], gp["mod"], gs_["merged"], gs_["x"], gs_["mod"], wo_b, norm2[:, None, :],
            w_router, b_router, l, t_s=gs_["t"], name=f"outproj{l}")
        y_all = _moe_call(h2_all, grp_all, w_router, b_router, w_e_gate, w_e_up, w_e_down, l, name=f"moe{l}")
        y_prev = y_all
        for g in groups:
            g["res"] = (y_all, g["row0"], g["mod"])
    outs = []
    for g in groups:
        yn = _norm_call(g["x"], final_norm, None, g["res"], t_batch=g["tt"], name=f"final_{g['tag']}")
        outs.append(yn.reshape(g["nb"], g["t"], d))
    new_k = kv_new[0].reshape(bp, depth, tp, N_HEADS_A, 2, D_HEAD_A)
    new_v = kv_new[1].reshape(bp, depth, tp, N_HEADS_A, D_V_A)
    return (outs[0], outs[1], new_k, new_v, jnp.stack(ss, axis=1))
```

```python
import functools
import math

import jax
import jax.numpy as jnp
from jax import lax
from jax.experimental import pallas as pl
from jax.experimental.pallas import tpu as pltpu

F32 = jnp.float32
BF16 = jnp.bfloat16

GRID_W = 64
CHUNK = 128
N_HEADS_A = 8
D_HEAD_A = 64
D_V_A = 128
N_GROUPS_B = 8
D_GROUP_B = 128
N_HEADS_R = 8
D_K_R = 64
D_V_R = 128
ROPE_DIM = 64
ROPE_BASE = 10000.0
N_EXPERT_GROUPS = 4
EXPERTS_PER_GROUP = 4
N_EXPERTS = 16
D_EXPERT = 256
N_MOD = 6
EPS = 1e-6

W_QA = N_HEADS_A * 2 * D_HEAD_A
W_VA = N_HEADS_A * D_V_A
W_B = N_GROUPS_B * D_GROUP_B
W_QR = N_HEADS_R * D_K_R
W_VR = N_HEADS_R * D_V_R

OFF_QA = 0
OFF_KA = OFF_QA + W_QA
OFF_VA = OFF_KA + W_QA
OFF_U = OFF_VA + W_VA
OFF_V = OFF_U + W_B
OFF_QR = OFF_V + W_B
OFF_KR = OFF_QR + W_QR
OFF_VR = OFF_KR + W_QR
OFF_GR = OFF_VR + W_VR
OFF_GA = OFF_GR + W_VR

LANES = 128
RET_TILE = 256
ROUTER_LANES = 128
VMEM_LIMIT = 56 * 1024 * 1024


def _params(n_axes, vmem=None):
    return pltpu.CompilerParams(dimension_semantics=("arbitrary",) * n_axes, vmem_limit_bytes=vmem)


def _sigmoid(x):
    return 1.0 / (1.0 + jnp.exp(-x))


def _silu(x):
    return x * _sigmoid(x)


def _gelu_tanh(x):
    return 0.5 * x * (1.0 + jnp.tanh(math.sqrt(2.0 / math.pi) * (x + 0.044715 * (x * x * x))))


def _rms(x):
    return x * lax.rsqrt(jnp.mean(x * x, axis=-1, keepdims=True) + EPS)


def _mod_body(c_ref, w_ref, b_ref, o_ref):
    a = _silu(c_ref[...]).astype(BF16)
    o_ref[...] = jnp.dot(a, w_ref[...].astype(BF16), preferred_element_type=F32) + b_ref[...]


def _mod_call(cond8, w_mod, b_mod):
    depth, d, n6 = w_mod.shape
    tn = 1024
    return pl.pallas_call(
        _mod_body,
        grid=(depth, n6 // tn),
        in_specs=[
            pl.BlockSpec((8, d), lambda l, j: (0, 0)),
            pl.BlockSpec((None, d, tn), lambda l, j: (l, 0, j)),
            pl.BlockSpec((None, 1, tn), lambda l, j: (l, 0, j)),
        ],
        out_specs=pl.BlockSpec((None, 8, tn), lambda l, j: (l, 0, j)),
        out_shape=jax.ShapeDtypeStruct((depth, 8, n6), F32),
        compiler_params=_params(2, VMEM_LIMIT),
        name="mod_vectors",
    )(cond8, w_mod, b_mod.reshape(depth, 1, n6))


def _norm_body(*refs, has_res, adaln):
    it = iter(refs)
    x_ref = next(it)
    if has_res:
        y_ref = next(it)
        modp_ref = next(it)
    g_ref = next(it)
    if adaln:
        mod_ref = next(it)
    if has_res and adaln:
        xo_ref = next(it)
    h_ref = next(it)

    x = x_ref[...]
    if has_res:
        x = x + modp_ref[5:6, :] * y_ref[...]
        if adaln:
            xo_ref[...] = x
    y = _rms(x) * g_ref[...]
    if adaln:
        y = y * (1.0 + mod_ref[1:2, :]) + mod_ref[0:1, :]
    h_ref[...] = y.astype(h_ref.dtype)


def _norm_call(x, gain, mod=None, res=None, *, t_batch, name):
    n, d = x.shape
    tm = min(512, t_batch)
    per = t_batch // tm
    adaln = mod is not None
    has_res = res is not None
    row = pl.BlockSpec((tm, d), lambda i: (i, 0))
    modspec = pl.BlockSpec((None, 8, d), lambda i: (i // per, 0, 0))
    args, specs = [x], [row]
    if has_res:
        y_all, row0, mod_prev = res
        assert row0 % tm == 0
        args += [y_all, mod_prev]
        specs += [pl.BlockSpec((tm, d), lambda i: (i + row0 // tm, 0)), modspec]
    args.append(gain.reshape(1, d))
    specs.append(pl.BlockSpec((1, d), lambda i: (0, 0)))
    if adaln:
        args.append(mod)
        specs.append(modspec)
    out_shape, out_specs = [], []
    if has_res and adaln:
        out_shape.append(jax.ShapeDtypeStruct((n, d), F32))
        out_specs.append(row)
    out_shape.append(jax.ShapeDtypeStruct((n, d), BF16 if adaln else F32))
    out_specs.append(row)
    outs = pl.pallas_call(
        functools.partial(_norm_body, has_res=has_res, adaln=adaln),
        grid=(n // tm,),
        in_specs=specs,
        out_specs=out_specs,
        out_shape=out_shape,
        compiler_params=_params(1, VMEM_LIMIT),
        name=name,
    )(*args)
    if has_res and adaln:
        return outs[0], outs[1]
    return (x, outs[0]) if adaln else outs[0]


def _norm2_body(*refs, has_res, steps_p):
    it = iter(refs)
    x_p, x_s = next(it), next(it)
    if has_res:
        y_ref, modp_p, modp_s = next(it), next(it), next(it)
    g_ref, mod_p, mod_s = next(it), next(it), next(it)
    if has_res:
        xo_p, xo_s = next(it), next(it)
    h_ref = next(it)

    def run(x_ref, mod_ref, modp_ref, xo_ref):
        x = x_ref[...]
        if has_res:
            x = x + modp_ref[5:6, :] * y_ref[...]
            xo_ref[...] = x
        h_ref[...] = ((_rms(x) * g_ref[...]) * (1.0 + mod_ref[1:2, :]) + mod_ref[0:1, :]).astype(h_ref.dtype)

    i = pl.program_id(0)

    @pl.when(i < steps_p)
    def _():
        run(x_p, mod_p, modp_p if has_res else None, xo_p if has_res else None)

    @pl.when(i >= steps_p)
    def _():
        run(x_s, mod_s, modp_s if has_res else None, xo_s if has_res else None)


def _norm2_call(x_p, x_s, gain, mod_p, mod_s, res, *, t_s, name):
    n_p, d = x_p.shape
    n_s = x_s.shape[0]
    tm = min(512, n_p, t_s)
    assert n_p % tm == 0 and t_s % tm == 0
    sp, ss = n_p // tm, n_s // tm
    per_s = t_s // tm
    has_res = res is not None
    row_p = pl.BlockSpec((tm, d), lambda i: (jnp.minimum(i, sp - 1), 0))
    row_s = pl.BlockSpec((tm, d), lambda i: (jnp.maximum(i - sp, 0), 0))
    row_all = pl.BlockSpec((tm, d), lambda i: (i, 0))
    m_p = pl.BlockSpec((None, 8, d), lambda i: (0, 0, 0))
    m_s = pl.BlockSpec((None, 8, d), lambda i: (jnp.maximum(i - sp, 0) // per_s, 0, 0))
    args, specs = [x_p, x_s], [row_p, row_s]
    if has_res:
        args += list(res)
        specs += [row_all, m_p, m_s]
    args += [gain.reshape(1, d), mod_p, mod_s]
    specs += [pl.BlockSpec((1, d), lambda i: (0, 0)), m_p, m_s]
    out_shape, out_specs = [], []
    if has_res:
        out_shape += [jax.ShapeDtypeStruct((n_p, d), F32), jax.ShapeDtypeStruct((n_s, d), F32)]
        out_specs += [row_p, row_s]
    out_shape.append(jax.ShapeDtypeStruct((n_p + n_s, d), BF16))
    out_specs.append(row_all)
    outs = pl.pallas_call(
        functools.partial(_norm2_body, has_res=has_res, steps_p=sp),
        grid=(sp + ss,),
        in_specs=specs,
        out_specs=out_specs,
        out_shape=out_shape,
        compiler_params=_params(1, VMEM_LIMIT),
        name=name,
    )(*args)
    return (outs[0], outs[1], outs[2]) if has_res else (x_p, x_s, outs[0])


def _rope_swap(blk):
    lane = lax.broadcasted_iota(jnp.int32, blk.shape, 1)
    return jnp.where((lane & 32) == 0, pltpu.roll(blk, LANES - 32, 1), pltpu.roll(blk, 32, 1))


def _inproj_body(h_ref, w_ref, cs_ref, sn_ref, o_ref, wbf_ref, *, tiles_p, tn):
    n = pl.program_id(0)
    m = pl.program_id(1)

    @pl.when(m == 0)
    def _():
        wbf_ref[...] = w_ref[...].astype(BF16)

    groups = tn // LANES
    latent = m >= tiles_p
    qk_attn = n < (OFF_VA // tn)
    qk_ret = n == (OFF_QR // tn)

    def tile(rotate, scale_k):
        acc = jnp.dot(h_ref[...], wbf_ref[...], preferred_element_type=F32)
        if not rotate and not scale_k:
            o_ref[...] = acc
            return
        for j in range(groups):
            blk = acc[:, j * LANES:(j + 1) * LANES]
            if scale_k and j * LANES >= W_QR:
                blk = blk * (D_K_R ** -0.5)
            if rotate:
                blk = blk * cs_ref[...] + _rope_swap(blk) * sn_ref[...]
            o_ref[:, j * LANES:(j + 1) * LANES] = blk

    @pl.when(qk_attn & latent)
    def _():
        tile(True, False)

    @pl.when(qk_ret & latent)
    def _():
        tile(True, True)

    @pl.when(qk_ret & jnp.logical_not(latent))
    def _():
        tile(False, True)

    @pl.when(jnp.logical_not(qk_ret | (qk_attn & latent)))
    def _():
        tile(False, False)


def _inproj_call(h_all, w_in, layer, rope_tabs, *, n_p, t_s, name):
    n_tok, d = h_all.shape
    d_in = w_in.shape[-1]
    tn = 1024
    assert OFF_QR % tn == 0 and OFF_QR + 2 * W_QR == OFF_QR + tn and d_in % tn == 0
    tm = min(1024, n_p, t_s)
    assert n_p % tm == 0 and t_s % tm == 0
    tiles_p, per = n_p // tm, t_s // tm
    tab = pl.BlockSpec((tm, LANES), lambda n, m: (jnp.maximum(m - tiles_p, 0) % per, 0))
    return pl.pallas_call(
        functools.partial(_inproj_body, tiles_p=tiles_p, tn=tn),
        grid=(d_in // tn, n_tok // tm),
        in_specs=[
            pl.BlockSpec((tm, d), lambda n, m: (m, 0)),
            pl.BlockSpec((None, d, tn), lambda n, m: (layer, 0, n)),
            tab, tab,
        ],
        out_specs=pl.BlockSpec((tm, tn), lambda n, m: (m, n)),
        out_shape=jax.ShapeDtypeStruct((n_tok, d_in), F32),
        scratch_shapes=[pltpu.VMEM((d, tn), BF16)],
        compiler_params=_params(2, VMEM_LIMIT),
        name=name,
    )(h_all, w_in, *rope_tabs)


SHIFT_SAFE = 40.0


def _attn_body(*refs, hb, t_new, past, lam_init, emit_kv, n_prev, bounded):
    it = iter(refs)
    lp_ref, q_ref, k_ref, v_ref = next(it), next(it), next(it), next(it)
    if past:
        ck_ref, cv_ref = next(it), next(it)
    g_ref = next(it)
    if emit_kv and n_prev:
        kp_ref, vp_ref = next(it), next(it)
    o_ref = next(it)
    if emit_kv:
        ko_ref, vo_ref = next(it), next(it)
    kbf, vbf = next(it), next(it)
    if bounded:
        kmax = next(it)
    i = pl.program_id(2)

    def map_mask(shape, mp):
        lane = lax.broadcasted_iota(jnp.int32, shape, 1)
        return (lane % LANES >= mp * D_HEAD_A) & (lane % LANES < (mp + 1) * D_HEAD_A)

    @pl.when(i == 0)
    def _():
        if past:
            kbf[0:past, :] = ck_ref[...].astype(BF16)
            vbf[0:past, :] = cv_ref[...].astype(BF16)
        kbf[past:past + t_new, :] = k_ref[...].astype(BF16)
        vbf[past:past + t_new, :] = v_ref[...].astype(BF16)
        if emit_kv:
            if n_prev:
                ko_ref[0:n_prev] = kp_ref[...]
                vo_ref[0:n_prev] = vp_ref[...]
            ko_ref[n_prev] = k_ref[...]
            vo_ref[n_prev] = v_ref[...]
        if bounded:
            k2 = kbf[...].astype(F32)
            k2 = k2 * k2
            for mp in range(2):
                n2 = jnp.sum(jnp.where(map_mask(k2.shape, mp), k2, 0.0), axis=-1, keepdims=True)
                kmax[mp] = jnp.broadcast_to(jnp.sqrt(jnp.max(n2, axis=0, keepdims=True)), kmax.shape[1:])

    lp = lp_ref[...]
    lam = (jnp.exp(jnp.sum(lp[0:1, :] * lp[1:2, :], keepdims=True))
           - jnp.exp(jnp.sum(lp[2:3, :] * lp[3:4, :], keepdims=True)) + lam_init)
    nt = (((1,), (1,)), ((), ()))

    def head(j, shifts):
        sl = slice(j * LANES, (j + 1) * LANES)
        k = kbf[:, sl]
        ps, rs = [], []
        for mp in range(2):
            s = lax.dot_general(qms[j][mp], k, nt, preferred_element_type=F32)
            c = jnp.max(s, axis=-1, keepdims=True) if shifts is None else shifts[mp]
            p = jnp.exp(s - c)
            ps.append(p)
            rs.append(jnp.sum(p, axis=-1, keepdims=True))
        a = (ps[0] * (1.0 / rs[0]) - ps[1] * (lam / rs[1])).astype(BF16)
        o = jnp.dot(a, vbf[:, sl], preferred_element_type=F32)
        o_ref[:, sl] = ((_rms(o) * g_ref[...]) * (1.0 - lam_init)).astype(o_ref.dtype)

    qms = []
    for j in range(hb):
        q = q_ref[:, j * LANES:(j + 1) * LANES] * (D_HEAD_A ** -0.5)
        qms.append([jnp.where(map_mask(q.shape, mp), q, 0.0).astype(BF16) for mp in range(2)])

    if not bounded:
        for j in range(hb):
            head(j, None)
        return

    assert hb == 1
    shifts = []
    for mp in range(2):
        qf = qms[0][mp].astype(F32)
        qn = jnp.sqrt(jnp.sum(qf * qf, axis=-1, keepdims=True))
        shifts.append(qn * kmax[mp][0:1, 0:1] * 1.001 + 1e-6)
    safe = jnp.max(jnp.maximum(shifts[0], shifts[1])) <= SHIFT_SAFE

    @pl.when(safe)
    def _():
        head(0, shifts)

    @pl.when(jnp.logical_not(safe))
    def _():
        head(0, None)


def _attn_call(proj, lambda_p, diff_norm, cache, layer, *, row0, n_batch, t_batch, lam_init, hb, emit_kv, kv_prev,
               name):
    n_tok = n_batch * t_batch
    tq = min(256, t_batch)
    qsteps = t_batch // tq
    assert row0 % t_batch == 0
    q0, b0 = row0 // tq, row0 // t_batch
    w = hb * LANES
    past = 0 if cache is None else cache[0].shape[2]
    assert not emit_kv or (w == W_QA and qsteps == 1)
    args = [lambda_p, proj, proj, proj]
    specs = [
        pl.BlockSpec((None, 4, D_HEAD_A), lambda b, h, i: (layer, 0, 0)),
        pl.BlockSpec((tq, w), lambda b, h, i: (q0 + b * qsteps + i, OFF_QA // w + h)),
        pl.BlockSpec((t_batch, w), lambda b, h, i: (b0 + b, OFF_KA // w + h)),
        pl.BlockSpec((t_batch, w), lambda b, h, i: (b0 + b, OFF_VA // w + h)),
    ]
    if past:
        cspec = pl.BlockSpec((None, None, past, w), lambda b, h, i: (b, layer, 0, h))
        args += [cache[0], cache[1]]
        specs += [cspec, cspec]
    args.append(diff_norm)
    specs.append(pl.BlockSpec((None, 1, D_V_A), lambda b, h, i: (layer, 0, 0)))
    t_keys = past + t_batch
    out_specs = [pl.BlockSpec((tq, w), lambda b, h, i: (b * qsteps + i, h))]
    out_shape = [jax.ShapeDtypeStruct((n_tok, W_VA), BF16)]
    n_prev = 0
    if emit_kv:
        n_prev = 0 if kv_prev is None else kv_prev[0].shape[1]
        if n_prev:
            args += list(kv_prev)
            specs += [pl.BlockSpec((None, n_prev, t_batch, w), lambda b, h, i: (b, 0, 0, 0))] * 2
        out_specs += [pl.BlockSpec((None, n_prev + 1, t_batch, w), lambda b, h, i: (b, 0, 0, 0))] * 2
        out_shape += [jax.ShapeDtypeStruct((n_batch, n_prev + 1, t_batch, w), F32)] * 2
    bounded = hb == 1
    scratch = [pltpu.VMEM((t_keys, w), BF16), pltpu.VMEM((t_keys, w), BF16)]
    if bounded:
        scratch.append(pltpu.VMEM((2, 8, LANES), F32))
    outs = pl.pallas_call(
        functools.partial(_attn_body, hb=hb, t_new=t_batch, past=past, lam_init=lam_init, emit_kv=emit_kv,
                          n_prev=n_prev, bounded=bounded),
        grid=(n_batch, N_HEADS_A // hb, qsteps),
        in_specs=specs,
        out_specs=out_specs,
        out_shape=out_shape,
        scratch_shapes=scratch,
        compiler_params=_params(3, VMEM_LIMIT),
        name=name,
    )(*args)
    return outs if emit_kv else outs[0]


def _ret_body(*refs, hb, nc, c_, has_state):
    if has_state:
        lg_ref, q_ref, k_ref, v_ref, s0_ref, of_ref, ob_ref, sf_ref, st_ref, dec_ref, vec_ref, cd_ref = refs
    else:
        lg_ref, q_ref, k_ref, v_ref, of_ref, ob_ref, sf_ref, st_ref, dec_ref, vec_ref, cd_ref = refs
    hblk = pl.program_id(1)
    half = D_K_R

    rel = (lax.broadcasted_iota(jnp.int32, (c_, c_), 0) - lax.broadcasted_iota(jnp.int32, (c_, c_), 1)).astype(F32)
    pos = lax.broadcasted_iota(jnp.int32, (c_, LANES), 0).astype(F32)
    for j in range(hb):
        for d in range(2):
            lg = lg_ref[d, hblk * hb + j]
            if d == 0:
                dec_ref[d, j] = jnp.where(rel >= 0, jnp.exp(jnp.maximum(rel, 0.0) * lg), 0.0)
                vec_ref[d, j, 0] = jnp.exp((pos + 1.0) * lg)
                vec_ref[d, j, 1] = jnp.exp((c_ - 1.0 - pos) * lg)
            else:
                dec_ref[d, j] = jnp.where(rel <= 0, jnp.exp(jnp.maximum(-rel, 0.0) * lg), 0.0)
                vec_ref[d, j, 0] = jnp.exp((c_ - pos) * lg)
                vec_ref[d, j, 1] = jnp.exp(pos * lg)
            cd_ref[d, j] = jnp.exp(jnp.full((LANES, D_V_R), float(c_), F32) * lg)
            st_ref[d, j] = jnp.zeros((LANES, D_V_R), F32)
            if has_state:
                lo = (j % 2) * half
                st_ref[d, j, lo:lo + half, :] = s0_ref[d, j]

    nt = (((1,), (1,)), ((), ()))

    def step(i, carry):
        for d in range(2):
            c = i if d == 0 else nc - 1 - i
            r0 = pl.multiple_of(c * c_, c_)
            for j in range(hb):
                jp = j // 2
                q = q_ref[pl.ds(r0, c_), jp * LANES:(jp + 1) * LANES]
                k = k_ref[pl.ds(r0, c_), jp * LANES:(jp + 1) * LANES]
                v = v_ref[pl.ds(r0, c_), j * D_V_R:(j + 1) * D_V_R].astype(BF16)
                lane = lax.broadcasted_iota(jnp.int32, q.shape, 1)
                mine = (lane >= (j % 2) * half) & (lane < (j % 2 + 1) * half)
                qm = jnp.where(mine, q, 0.0).astype(BF16)
                s = lax.dot_general(qm, k.astype(BF16), nt, preferred_element_type=F32) * dec_ref[d, j]
                slab = st_ref[d, j]
                o = (jnp.dot(s.astype(BF16), v, preferred_element_type=F32)
                     + jnp.dot(qm, slab.astype(BF16), preferred_element_type=F32) * vec_ref[d, j, 0])
                kzt = (k * vec_ref[d, j, 1]).T.astype(BF16)
                st_ref[d, j] = slab * cd_ref[d, j] + jnp.dot(kzt, v, preferred_element_type=F32)
                if d == 0:
                    of_ref[pl.ds(r0, c_), j * D_V_R:(j + 1) * D_V_R] = o
                else:
                    ob_ref[pl.ds(r0, c_), j * D_V_R:(j + 1) * D_V_R] = o
        return carry

    lax.fori_loop(0, nc, step, 0, unroll=4 if nc % 4 == 0 else 1)
    for j in range(hb):
        lo = (j % 2) * half
        for d in range(2):
            sf_ref[d, j] = st_ref[d, j, lo:lo + half, :]


def _ret_call(proj, log_g, state, layer, *, row0, n_batch, t_batch, hb, name):
    n_tok = n_batch * t_batch
    assert row0 % t_batch == 0
    b0 = row0 // t_batch
    c_ = min(RET_TILE, t_batch)
    nc = t_batch // c_
    wq, wv = hb * D_K_R, hb * D_V_R
    args = [log_g, proj, proj, proj]
    specs = [
        pl.BlockSpec(memory_space=pltpu.SMEM),
        pl.BlockSpec((t_batch, wq), lambda b, h: (b0 + b, OFF_QR // wq + h)),
        pl.BlockSpec((t_batch, wq), lambda b, h: (b0 + b, OFF_KR // wq + h)),
        pl.BlockSpec((t_batch, wv), lambda b, h: (b0 + b, OFF_VR // wv + h)),
    ]
    has_state = state is not None
    if has_state:
        args.append(state)
        specs.append(pl.BlockSpec((None, None, 2, hb, D_K_R, D_V_R), lambda b, h: (b, layer, 0, h, 0, 0)))
    ospec = pl.BlockSpec((t_batch, wv), lambda b, h: (b, h))
    return pl.pallas_call(
        functools.partial(_ret_body, hb=hb, nc=nc, c_=c_, has_state=has_state),
        grid=(n_batch, N_HEADS_R // hb),
        in_specs=specs,
        out_specs=[ospec, ospec, pl.BlockSpec((None, 2, hb, D_K_R, D_V_R), lambda b, h: (b, 0, h, 0, 0))],
        out_shape=[
            jax.ShapeDtypeStruct((n_tok, W_VR), F32),
            jax.ShapeDtypeStruct((n_tok, W_VR), F32),
            jax.ShapeDtypeStruct((n_batch, 2, N_HEADS_R, D_K_R, D_V_R), F32),
        ],
        scratch_shapes=[
            pltpu.VMEM((2, hb, LANES, D_V_R), F32),
            pltpu.VMEM((2, hb, c_, c_), F32),
            pltpu.VMEM((2, hb, 2, c_, LANES), F32),
            pltpu.VMEM((2, hb, LANES, D_V_R), F32),
        ],
        compiler_params=_params(2, VMEM_LIMIT),
        name=name,
    )(*args)


def _mix_body(oa_ref, of_ref, ob_ref, u_ref, v_ref, gr_ref, ga_ref, gb_ref, gc_ref, sgn_ref, sgw_ref, sgb_ref,
              rn_ref, wa_ref, wb_ref, wc_ref, o_ref, sb_s, oc_s, *, tm):
    for j in range(N_HEADS_R):
        sl = slice(j * D_V_R, (j + 1) * D_V_R)
        y = _rms(of_ref[:, sl] + ob_ref[:, sl]) * rn_ref[:, sl]
        oc_s[:, sl] = (_silu(gr_ref[:, sl]) * y).astype(BF16)
    gv = _gelu_tanh(v_ref[...])
    vn = (_rms(gv) * sgn_ref[...]).astype(BF16)
    for g in range(N_GROUPS_B):
        sl = slice(g * D_GROUP_B, (g + 1) * D_GROUP_B)
        wg = sgw_ref[g].astype(BF16)
        bias = sgb_ref[:, g:g + 1]
        for c in range(tm // CHUNK):
            rows = slice(c * CHUNK, (c + 1) * CHUNK)
            mixed = jnp.dot(wg, vn[rows, sl], preferred_element_type=F32) + bias
            sb_s[rows, sl] = (_gelu_tanh(u_ref[rows, sl]) * mixed).astype(BF16)

    ya = jnp.dot(oa_ref[...], wa_ref[...], preferred_element_type=F32)
    yb = jnp.dot(sb_s[...], wb_ref[...], preferred_element_type=F32)
    yc = jnp.dot(oc_s[...], wc_ref[...], preferred_element_type=F32)
    merged = _sigmoid(ga_ref[...]) * ya + _sigmoid(gb_ref[...]) * yb + _sigmoid(gc_ref[...]) * yc
    o_ref[...] = merged.astype(o_ref.dtype)


def _mix_call(oa, o_f, o_b, proj, sg_norm, sg_w, sg_bt, ret_norm, wa, wb, wc, layer, *, row0, t_batch, name):
    n_tok = oa.shape[0]
    d = wa.shape[-1]
    tm = min(256, t_batch)
    assert row0 % tm == 0
    m0 = row0 // tm
    wide = pl.BlockSpec((tm, W_B), lambda m: (m, 0))

    def pcol(off, width):
        return pl.BlockSpec((tm, width), lambda m: (m0 + m, off // width))

    def lvec(width):
        return pl.BlockSpec((None, 1, width), lambda m: (layer, 0, 0))

    wspec = pl.BlockSpec((None, W_B, d), lambda m: (layer, 0, 0), pipeline_mode=pl.Buffered(1))
    return pl.pallas_call(
        functools.partial(_mix_body, tm=tm),
        grid=(n_tok // tm,),
        in_specs=[
            wide, wide, wide, pcol(OFF_U, W_B), pcol(OFF_V, W_B), pcol(OFF_GR, W_VR),
            pcol(OFF_GA, d), pcol(OFF_GA + d, d), pcol(OFF_GA + 2 * d, d),
            lvec(W_B),
            pl.BlockSpec((None, N_GROUPS_B, CHUNK, CHUNK), lambda m: (layer, 0, 0, 0)),
            pl.BlockSpec((None, CHUNK, N_GROUPS_B), lambda m: (layer, 0, 0)),
            lvec(W_VR), wspec, wspec, wspec,
        ],
        out_specs=pl.BlockSpec((tm, d), lambda m: (m, 0)),
        out_shape=jax.ShapeDtypeStruct((n_tok, d), BF16),
        scratch_shapes=[pltpu.VMEM((tm, W_B), BF16), pltpu.VMEM((tm, W_VR), BF16)],
        compiler_params=_params(1, VMEM_LIMIT),
        name=name,
    )(oa, o_f, o_b, proj, proj, proj, proj, proj, proj, sg_norm, sg_w, sg_bt, ret_norm, wa, wb, wc)


def _route(logits):
    lane = lax.broadcasted_iota(jnp.int32, logits.shape, 1).astype(F32)
    big = float(1 << 20)
    neg = -jnp.inf
    lgm = jnp.where(lane < N_EXPERT_GROUPS, logits, neg)
    mx = jnp.max(lgm, axis=-1, keepdims=True)
    p_group = 1.0 / jnp.sum(jnp.exp(lgm - mx), axis=-1, keepdims=True)
    gsel = jnp.min(jnp.where(lgm == mx, lane, big), axis=-1, keepdims=True)
    lo = N_EXPERT_GROUPS + gsel * EXPERTS_PER_GROUP
    insel = (lane >= lo) & (lane < lo + EXPERTS_PER_GROUP)
    le = jnp.where(insel, logits, neg)
    v1 = jnp.max(le, axis=-1, keepdims=True)
    i1 = jnp.min(jnp.where(le == v1, lane, big), axis=-1, keepdims=True)
    le2 = jnp.where(lane == i1, neg, le)
    v2 = jnp.max(le2, axis=-1, keepdims=True)
    i2 = jnp.min(jnp.where(le2 == v2, lane, big), axis=-1, keepdims=True)
    e2 = jnp.exp(v2 - v1)
    den = 1.0 + e2
    w1 = p_group * (1.0 / den)
    w2 = p_group * (e2 / den)
    return jnp.where(lane == i1, w1, 0.0) + jnp.where(lane == i2, w2, 0.0), gsel


def _outproj_body(mg_p, x_p, mod_p, mg_s, x_s, mod_s, w_ref, g_ref, wr_ref, br_ref, x1_p, x1_s, h2_ref, rt_ref, *,
                  steps_p):
    def run(mg_ref, x_ref, mod_ref, x1_ref):
        y = jnp.dot(mg_ref[...], w_ref[...], preferred_element_type=F32)
        x1 = x_ref[...] + mod_ref[2:3, :] * y
        x1_ref[...] = x1
        h2 = (_rms(x1) * g_ref[...]) * (1.0 + mod_ref[4:5, :]) + mod_ref[3:4, :]
        h2_ref[...] = h2
        logits = jnp.dot(h2.astype(BF16), wr_ref[...], preferred_element_type=F32) + br_ref[...]
        _, gsel = _route(logits)
        rt_ref[...] = jnp.broadcast_to(gsel, rt_ref.shape)

    i = pl.program_id(0)

    @pl.when(i < steps_p)
    def _():
        run(mg_p, x_p, mod_p, x1_p)

    @pl.when(i >= steps_p)
    def _():
        run(mg_s, x_s, mod_s, x1_s)


def _outproj_call(merged_p, x_p, mod_p, merged_s, x_s, mod_s, w_out, norm2, w_router, b_router, layer, *, t_s, name):
    n_p, d = x_p.shape
    n_s = x_s.shape[0]
    tm = 256
    assert n_p % tm == 0 and t_s % tm == 0
    sp, ss = n_p // tm, n_s // tm
    per_s = t_s // tm
    row_p = pl.BlockSpec((tm, d), lambda i: (jnp.minimum(i, sp - 1), 0))
    row_s = pl.BlockSpec((tm, d), lambda i: (jnp.maximum(i - sp, 0), 0))
    return pl.pallas_call(
        functools.partial(_outproj_body, steps_p=sp),
        grid=(sp + ss,),
        in_specs=[
            row_p, row_p, pl.BlockSpec((None, 8, d), lambda i: (0, 0, 0)),
            row_s, row_s, pl.BlockSpec((None, 8, d), lambda i: (jnp.maximum(i - sp, 0) // per_s, 0, 0)),
            pl.BlockSpec((None, d, d), lambda i: (layer, 0, 0)),
            pl.BlockSpec((None, 1, d), lambda i: (layer, 0, 0)),
            pl.BlockSpec((None, d, ROUTER_LANES), lambda i: (layer, 0, 0)),
            pl.BlockSpec((None, 1, ROUTER_LANES), lambda i: (layer, 0, 0)),
        ],
        out_specs=[row_p, row_s, pl.BlockSpec((tm, d), lambda i: (i, 0)),
                   pl.BlockSpec((tm, ROUTER_LANES), lambda i: (i, 0))],
        out_shape=[
            jax.ShapeDtypeStruct((n_p, d), F32),
            jax.ShapeDtypeStruct((n_s, d), F32),
            jax.ShapeDtypeStruct((n_p + n_s, d), F32),
            jax.ShapeDtypeStruct((n_p + n_s, ROUTER_LANES), F32),
        ],
        compiler_params=_params(1, VMEM_LIMIT),
        name=name,
    )(merged_p, x_p, mod_p, merged_s, x_s, mod_s, w_out, norm2, w_router, b_router)


MOE_FIRST, MOE_LAST, MOE_VALID = 1, 2, 4


def _moe_plan(gsel, tm):
    n = gsel.shape[0]
    nt = n // tm
    n_items = nt + N_EXPERT_GROUPS - 1
    perm = jnp.argsort(gsel, stable=True).astype(jnp.int32)
    gs = gsel[perm]
    gf, gl = gs[0::tm], gs[tm - 1::tm]
    grp = jnp.arange(N_EXPERT_GROUPS, dtype=jnp.int32)[None, :]
    active = ((grp >= gf[:, None]) & (grp <= gl[:, None])).reshape(-1)
    order = jnp.argsort(jnp.logical_not(active), stable=True).astype(jnp.int32)[:n_items]
    valid = active[order]
    last_real = order[jnp.sum(active.astype(jnp.int32)) - 1]
    item = jnp.where(valid, order, last_real)
    wt, wg = item // N_EXPERT_GROUPS, item % N_EXPERT_GROUPS
    flags = (jnp.where(valid, MOE_VALID, 0) + jnp.where(valid & (wg == gf[wt]), MOE_FIRST, 0)
             + jnp.where(valid & (wg == gl[wt]), MOE_LAST, 0)).astype(jnp.int32)
    return perm, wt.astype(jnp.int32), wg.astype(jnp.int32), flags


def _moe_body(perm_ref, wt_ref, wg_ref, fl_ref, h_hbm, wr_ref, br_ref, wgate_ref, wup_ref, wdown_ref, y_hbm,
              h32, hbf, acc, dws, sem, *, tm, n_tiles):
    w = pl.program_id(0)
    e = pl.program_id(1)
    flags = fl_ref[w]
    valid = (flags & MOE_VALID) != 0
    tile = wt_ref[w]
    base = tile * tm
    slot = tile % 2

    def row_in(t, sl, r):
        return pltpu.make_async_copy(h_hbm.at[pl.ds(perm_ref[t * tm + r], 1), :], h32.at[sl, pl.ds(r, 1), :],
                                     sem.at[sl])

    def row_out(r):
        return pltpu.make_async_copy(acc.at[pl.ds(r, 1), :], y_hbm.at[pl.ds(perm_ref[base + r], 1), :], sem.at[2])

    def for_rows(fn):
        def body(r8, c):
            for s in range(8):
                fn(pl.multiple_of(r8 * 8, 8) + s)
            return c
        lax.fori_loop(0, tm // 8, body, 0)

    def start_rows(copy):
        for r in range(tm):
            copy(r).start()

    @pl.when(valid & ((flags & MOE_FIRST) != 0) & (e == 0))
    def _():
        @pl.when(tile == 0)
        def _():
            start_rows(lambda r: row_in(0, 0, r))

        for_rows(lambda r: row_in(tile, slot, r).wait())

        @pl.when(tile + 1 < n_tiles)
        def _():
            start_rows(lambda r: row_in(tile + 1, 1 - slot, r))

        hb = h32[slot].astype(BF16)
        hbf[...] = hb
        logits = jnp.dot(hb, wr_ref[...], preferred_element_type=F32) + br_ref[...]
        dws[...], _ = _route(logits)
        acc[...] = jnp.zeros(acc.shape, F32)

    @pl.when(valid)
    def _():
        h = hbf[...]
        a = jnp.dot(h, wgate_ref[...].astype(BF16), preferred_element_type=F32)
        up = jnp.dot(h, wup_ref[...].astype(BF16), preferred_element_type=F32)
        lane = lax.broadcasted_iota(jnp.int32, dws.shape, 1)
        ex = N_EXPERT_GROUPS + wg_ref[w] * EXPERTS_PER_GROUP + e
        wgt = jnp.sum(jnp.where(lane == ex, dws[...], 0.0), axis=-1, keepdims=True)
        act = ((_silu(a) * up) * wgt).astype(BF16)
        acc[...] += jnp.dot(act, wdown_ref[...].astype(BF16), preferred_element_type=F32)

    @pl.when(valid & ((flags & MOE_LAST) != 0) & (e == EXPERTS_PER_GROUP - 1))
    def _():
        start_rows(row_out)
        for_rows(lambda r: row_out(r).wait())


def _moe_call(h2_all, grp_all, w_router, b_router, w_gate, w_up, w_down, layer, *, name):
    n, d = h2_all.shape
    f = w_gate.shape[-1]
    tm = 1024 if n % 1024 == 0 and n >= 2048 else 256
    perm, wt, wg, flags = _moe_plan(grp_all[:, 0].astype(jnp.int32), tm)
    n_items = wt.shape[0]

    def expert(w, e, perm_ref, wt_ref, wg_ref, fl_ref):
        e_eff = jnp.where((fl_ref[w] & MOE_VALID) != 0, e, EXPERTS_PER_GROUP - 1)
        return (layer, wg_ref[w] * EXPERTS_PER_GROUP + e_eff, 0, 0)

    return pl.pallas_call(
        functools.partial(_moe_body, tm=tm, n_tiles=n // tm),
        grid_spec=pltpu.PrefetchScalarGridSpec(
            num_scalar_prefetch=4,
            grid=(n_items, EXPERTS_PER_GROUP),
            in_specs=[
                pl.BlockSpec(memory_space=pl.ANY),
                pl.BlockSpec((None, d, ROUTER_LANES), lambda w, e, *_: (layer, 0, 0)),
                pl.BlockSpec((None, 1, ROUTER_LANES), lambda w, e, *_: (layer, 0, 0)),
                pl.BlockSpec((None, None, d, f), expert),
                pl.BlockSpec((None, None, d, f), expert),
                pl.BlockSpec((None, None, f, d), expert),
            ],
            out_specs=pl.BlockSpec(memory_space=pl.ANY),
            scratch_shapes=[
                pltpu.VMEM((2, tm, d), F32),
                pltpu.VMEM((tm, d), BF16),
                pltpu.VMEM((tm, d), F32),
                pltpu.VMEM((tm, ROUTER_LANES), F32),
                pltpu.SemaphoreType.DMA((3,)),
            ],
        ),
        out_shape=jax.ShapeDtypeStruct((n, d), F32),
        compiler_params=_params(2, VMEM_LIMIT),
        name=name,
    )(perm, wt, wg, flags, h2_all, w_router, b_router, w_gate, w_up, w_down)


def _rope_tables(n_tok):
    rows = n_tok // GRID_W
    row = jnp.repeat(jnp.arange(rows, dtype=F32), GRID_W)
    col = jnp.tile(jnp.arange(GRID_W, dtype=F32), rows)
    n_freq = ROPE_DIM // 4
    inv = ROPE_BASE ** (-jnp.arange(n_freq, dtype=F32) / n_freq)
    ang = jnp.concatenate([row[:, None] * inv, col[:, None] * inv], axis=-1)
    cos, sin = jnp.cos(ang), jnp.sin(ang)
    return jnp.concatenate([cos, cos, cos, cos], axis=-1), jnp.concatenate([-sin, sin, -sin, sin], axis=-1)


def kernel(x_prompt, x_sample, cache_k, cache_v, state_ret, c, c_ctx, w_mod, b_mod, norm1, w_in, lambda_p, diff_norm, sg_norm, sg_w, sg_b, ret_decay, ret_norm, w_up_a, w_up_b, w_up_c, w_out, norm2, w_rg, b_rg, w_re, b_re, w_e_gate, w_e_up, w_e_down, final_norm):
    depth = w_in.shape[0]
    bp, tp, d = x_prompt.shape
    bs, ts, _ = x_sample.shape
    past = cache_k.shape[2]

    cond8 = jnp.zeros((8, d), F32).at[0].set(c_ctx).at[1:1 + bs].set(c)
    mods = _mod_call(cond8, w_mod, b_mod).reshape(depth, 8, N_MOD, d)
    mods = jnp.pad(mods, ((0, 0), (0, 0), (0, 8 - N_MOD), (0, 0)))

    wa_b, wb_b, wc_b, wo_b = (w.astype(BF16) for w in (w_up_a, w_up_b, w_up_c, w_out))
    w_router = jnp.concatenate(
        [w_rg, w_re, jnp.zeros((depth, d, ROUTER_LANES - N_EXPERT_GROUPS - N_EXPERTS), F32)], axis=-1).astype(BF16)
    b_router = jnp.concatenate(
        [b_rg, b_re, jnp.zeros((depth, ROUTER_LANES - N_EXPERT_GROUPS - N_EXPERTS), F32)], axis=-1)[:, None, :]
    log_g = jax.nn.log_sigmoid(ret_decay.astype(F32))
    sg_bt = jnp.swapaxes(sg_b, 1, 2)
    rope_tabs = _rope_tables(ts)
    cache_k2 = cache_k.reshape(bs, depth, past, W_QA)
    cache_v2 = cache_v.reshape(bs, depth, past, W_VA)

    groups = [
        dict(tag="p", x=x_prompt.reshape(bp * tp, d), nb=bp, t=tp, tt=bp * tp, rope=None, ctx=False, hb_a=8, hb_r=8),
        dict(tag="s", x=x_sample.reshape(bs * ts, d), nb=bs, t=ts, tt=ts, rope=rope_tabs, ctx=True, hb_a=1, hb_r=2),
    ]
    kv_new, ss = None, []
    row0 = 0
    for g in groups:
        g["res"] = None
        g["row0"] = row0
        row0 += g["nb"] * g["t"]
    gp, gs_ = groups
    y_prev = None
    for l in range(depth):
        lam_init = 0.8 - 0.6 * math.exp(-0.3 * l)
        mod_p, mod_s = mods[l, 0:1], mods[l, 1:1 + bs]
        res = None if y_prev is None else (y_prev, gp["mod"], gs_["mod"])
        gp["x"], gs_["x"], h_all = _norm2_call(gp["x"], gs_["x"], norm1[l], mod_p, mod_s, res, t_s=ts,
                                               name=f"norm1_{l}")
        gp["mod"], gs_["mod"] = mod_p, mod_s
        proj = _inproj_call(h_all, w_in, l, rope_tabs, n_p=bp * tp, t_s=ts, name=f"inproj{l}")
        for g in groups:
            tag, nb, t, tt = g["tag"], g["nb"], g["t"], g["tt"]
            cache = (cache_k2, cache_v2) if g["ctx"] else None
            oa = _attn_call(proj, lambda_p, diff_norm[:, None, :], cache, l, row0=g["row0"], n_batch=nb, t_batch=t,
                            lam_init=lam_init, hb=g["hb_a"], emit_kv=not g["ctx"], kv_prev=kv_new,
                            name=f"attn_{tag}{l}")
            if not g["ctx"]:
                oa, kv_new = oa[0], (oa[1], oa[2])
            o_f, o_b, s_fin = _ret_call(proj, log_g[l], state_ret if g["ctx"] else None, l, row0=g["row0"],
                                        n_batch=nb, t_batch=t, hb=g["hb_r"], name=f"ret_{tag}{l}")
            g["merged"] = _mix_call(oa, o_f, o_b, proj, sg_norm[:, None, :], sg_w, sg_bt, ret_norm[:, None, :],
                                    wa_b, wb_b, wc_b, l, row0=g["row0"], t_batch=tt, name=f"mix_{tag}{l}")
            if not g["ctx"]:
                ss.append(s_fin)
        gp["x"], gs_["x"], h2_all, grp_all = _outproj_call(
            gp["merged"], gp["x"], gp["mod"], gs_["merged"], gs_["x"], gs_["mod"], wo_b, norm2[:, None, :],
            w_router, b_router, l, t_s=gs_["t"], name=f"outproj{l}")
        y_all = _moe_call(h2_all, grp_all, w_router, b_router, w_e_gate, w_e_up, w_e_down, l, name=f"moe{l}")
        y_prev = y_all
        for g in groups:
            g["res"] = (y_all, g["row0"], g["mod"])
    outs = []
    for g in groups:
        yn = _norm_call(g["x"], final_norm, None, g["res"], t_batch=g["tt"], name=f"final_{g['tag']}")
        outs.append(yn.reshape(g["nb"], g["t"], d))
    new_k = kv_new[0].reshape(bp, depth, tp, N_HEADS_A, 2, D_HEAD_A)
    new_v = kv_new[1].reshape(bp, depth, tp, N_HEADS_A, D_V_A)
    return (outs[0], outs[1], new_k, new_v, jnp.stack(ss, axis=1))
```
